```python
import math
import jax, jax.numpy as jnp
from jax import lax
import numpy as np

D_MODEL = 2048
BATCH = 4
SEQ = 4096
DEPTH = 2

CHUNK = 64
Q_BLOCK = 128
HEAD_DIM = 128
SB_HEADS = 6
SB_WIDTH = SB_HEADS * HEAD_DIM
DA_HEADS = 6
DA_HALF = HEAD_DIM // 2
DA_WIDTH = DA_HEADS * HEAD_DIM
CONV_WIDTH = D_MODEL - SB_WIDTH - DA_WIDTH
CONV_GROUPS = CONV_WIDTH // HEAD_DIM
CONV_KERNEL = 31
IN_WIDTH = 3 * SB_WIDTH + 2 * CONV_WIDTH + 3 * DA_WIDTH
D_FF = 5632
N_EXPERTS = 8
TOP_K = 2
D_FF_EXPERT = 2816
EPS = 1e-6
N_DENSE = (DEPTH + 1) // 2
N_MOE = DEPTH // 2

kernel_name = "hybrid_sb_conformer_diffattn_moe"


def rms_norm(x, g):
    xf = x.astype(jnp.float32)
    y = xf * lax.rsqrt(jnp.mean(xf * xf, axis=-1, keepdims=True) + EPS)
    return (y * g.astype(jnp.float32)).astype(x.dtype)


def to_heads(t, n_heads, d):
    b, s, _ = t.shape
    return t.reshape(b, s, n_heads, d).transpose(0, 2, 1, 3)


def stick_breaking_attention(q, k, v):
    seq = q.shape[2]
    scale = HEAD_DIM ** -0.5
    outs = []
    for q0 in range(0, seq, Q_BLOCK):
        kl = q0 + Q_BLOCK
        z = jnp.einsum('bhqd,bhkd->bhqk', q[:, :, q0:kl], k[:, :, :kl]).astype(jnp.float32) * scale
        t_pos = q0 + jnp.arange(Q_BLOCK)[:, None]
        s_pos = jnp.arange(kl)[None, :]
        valid = s_pos < t_pos
        log1m = jnp.where(valid, jax.nn.log_sigmoid(-z), 0.0)
        after = lax.cumsum(log1m, axis=3, reverse=True) - log1m
        w = jnp.where(valid, jnp.exp(jax.nn.log_sigmoid(z) + after), 0.0)
        outs.append(jnp.einsum('bhqk,bhkd->bhqd', w.astype(v.dtype), v[:, :, :kl]))
    return jnp.concatenate(outs, axis=2)


def differential_attention(q1, q2, k1, k2, v, lam):
    seq = q1.shape[2]
    scale = DA_HALF ** -0.5
    slopes = jnp.exp2(-8.0 * (jnp.arange(DA_HEADS, dtype=jnp.float32) + 1.0) / DA_HEADS)
    outs = []
    for q0 in range(0, seq, Q_BLOCK):
        kl = q0 + Q_BLOCK
        t_pos = q0 + jnp.arange(Q_BLOCK)[:, None]
        s_pos = jnp.arange(kl)[None, :]
        allowed = (s_pos // CHUNK) <= (t_pos // CHUNK)
        bias = -slopes[:, None, None] * jnp.abs(t_pos - s_pos).astype(jnp.float32)[None]

        def attn_map(qh, kh):
            sc = jnp.einsum('bhqd,bhkd->bhqk', qh[:, :, q0:kl], kh[:, :, :kl]).astype(jnp.float32) * scale
            return jax.nn.softmax(jnp.where(allowed, sc + bias, -jnp.inf), axis=-1)

        p = attn_map(q1, k1) - lam * attn_map(q2, k2)
        outs.append(jnp.einsum('bhqk,bhkd->bhqd', p.astype(v.dtype), v[:, :, :kl]))
    return jnp.concatenate(outs, axis=2)


def conformer_conv(a, g, conv_w, conv_b, ln_g, ln_b, w_pw):
    u = a * jax.nn.sigmoid(g)
    u = lax.conv_general_dilated(
        u, conv_w.reshape(CONV_KERNEL, 1, CONV_WIDTH).astype(u.dtype),
        window_strides=(1,), padding=[(CONV_KERNEL - 1, 0)],
        dimension_numbers=('NWC', 'WIO', 'NWC'), feature_group_count=CONV_WIDTH) + conv_b
    uf = u.astype(jnp.float32)
    mu = jnp.mean(uf, axis=-1, keepdims=True)
    var = jnp.mean(jnp.square(uf - mu), axis=-1, keepdims=True)
    un = ((uf - mu) * lax.rsqrt(var + EPS) * ln_g + ln_b).astype(u.dtype)
    return jax.nn.silu(un) @ w_pw


def swiglu(h, wg, wu, wd):
    return (jax.nn.silu(h @ wg) * (h @ wu)) @ wd


def moe_swiglu(h, w_router, e_gate, e_up, e_down):
    b, s, d = h.shape
    tok = h.reshape(b * s, d)
    logits = (tok @ w_router).astype(jnp.float32)
    top_val, top_idx = lax.top_k(logits, TOP_K)
    gates = jax.nn.softmax(top_val, axis=-1)
    combine = jnp.sum(jax.nn.one_hot(top_idx, N_EXPERTS, dtype=jnp.float32) * gates[..., None], axis=1)
    y = jnp.zeros_like(tok)
    for e in range(N_EXPERTS):
        y = y + combine[:, e:e + 1].astype(tok.dtype) * swiglu(tok, e_gate[e], e_up[e], e_down[e])
    return y.reshape(b, s, d)


def setup_inputs(seed: int = 0) -> dict:
    key = jax.random.key(seed)
    ks = jax.random.split(key, 24)
    f32 = jnp.float32
    nrm = lambda k, shape, scale: jax.random.normal(k, shape, f32) * scale
    return {
        "x": nrm(ks[0], (BATCH, SEQ, D_MODEL), 1.0),
        "attn_norm": 1.0 + nrm(ks[1], (DEPTH, D_MODEL), 0.05),
        "w_in": nrm(ks[2], (DEPTH, D_MODEL, IN_WIDTH), D_MODEL ** -0.5),
        "w_out": nrm(ks[3], (DEPTH, D_MODEL, D_MODEL), D_MODEL ** -0.5),
        "lam": nrm(ks[4], (DEPTH, 4, DA_HALF), 0.1),
        "diff_norm": 1.0 + nrm(ks[5], (DEPTH, DA_WIDTH), 0.05),
        "conv_w": nrm(ks[6], (DEPTH, CONV_KERNEL, CONV_WIDTH), CONV_KERNEL ** -0.5),
        "conv_b": nrm(ks[7], (DEPTH, CONV_WIDTH), 0.02),
        "conv_ln_g": 1.0 + nrm(ks[8], (DEPTH, CONV_WIDTH), 0.05),
        "conv_ln_b": nrm(ks[9], (DEPTH, CONV_WIDTH), 0.02),
        "w_conv_out": nrm(ks[10], (DEPTH, CONV_WIDTH, CONV_WIDTH), CONV_WIDTH ** -0.5),
        "ffn_norm": 1.0 + nrm(ks[11], (DEPTH, D_MODEL), 0.05),
        "w_gate": nrm(ks[12], (N_DENSE, D_MODEL, D_FF), D_MODEL ** -0.5),
        "w_up": nrm(ks[13], (N_DENSE, D_MODEL, D_FF), D_MODEL ** -0.5),
        "w_down": nrm(ks[14], (N_DENSE, D_FF, D_MODEL), D_FF ** -0.5),
        "w_router": nrm(ks[15], (N_MOE, D_MODEL, N_EXPERTS), D_MODEL ** -0.5),
        "e_gate": nrm(ks[16], (N_MOE, N_EXPERTS, D_MODEL, D_FF_EXPERT), D_MODEL ** -0.5),
        "e_up": nrm(ks[17], (N_MOE, N_EXPERTS, D_MODEL, D_FF_EXPERT), D_MODEL ** -0.5),
        "e_down": nrm(ks[18], (N_MOE, N_EXPERTS, D_FF_EXPERT, D_MODEL), D_FF_EXPERT ** -0.5),
        "final_norm": 1.0 + nrm(ks[19], (D_MODEL,), 0.05),
    }


def reference(x, attn_norm, w_in, w_out, lam, diff_norm, conv_w, conv_b, conv_ln_g, conv_ln_b,
              w_conv_out, ffn_norm, w_gate, w_up, w_down, w_router, e_gate, e_up, e_down, final_norm):
    b, s, _ = x.shape
    sizes = [SB_WIDTH] * 3 + [CONV_WIDTH] * 2 + [DA_WIDTH] * 3
    offs = []
    acc = 0
    for w in sizes[:-1]:
        acc += w
        offs.append(acc)
    for l in range(DEPTH):
        h = rms_norm(x, attn_norm[l])
        proj = h @ w_in[l]
        sb_q, sb_k, sb_v, glu_a, glu_g, da_q, da_k, da_v = jnp.split(proj, offs, axis=-1)

        sb_o = stick_breaking_attention(to_heads(sb_q, SB_HEADS, HEAD_DIM),
                                        to_heads(sb_k, SB_HEADS, HEAD_DIM),
                                        to_heads(sb_v, SB_HEADS, HEAD_DIM))
        sb_o = sb_o.transpose(0, 2, 1, 3).reshape(b, s, SB_WIDTH)

        cv_o = conformer_conv(glu_a, glu_g, conv_w[l], conv_b[l], conv_ln_g[l], conv_ln_b[l], w_conv_out[l])

        lam_init = 0.8 - 0.6 * math.exp(-0.3 * l)
        lv = lam[l].astype(jnp.float32)
        lam_full = jnp.exp(jnp.sum(lv[0] * lv[1])) - jnp.exp(jnp.sum(lv[2] * lv[3])) + lam_init
        qh = da_q.reshape(b, s, DA_HEADS, 2, DA_HALF).transpose(0, 2, 3, 1, 4)
        kh = da_k.reshape(b, s, DA_HEADS, 2, DA_HALF).transpose(0, 2, 3, 1, 4)
        da_o = differential_attention(qh[:, :, 0], qh[:, :, 1], kh[:, :, 0], kh[:, :, 1],
                                      to_heads(da_v, DA_HEADS, HEAD_DIM), lam_full)
        da_o = da_o.transpose(0, 2, 1, 3)
        da_o = rms_norm(da_o, diff_norm[l].reshape(DA_HEADS, HEAD_DIM)) * (1.0 - lam_init)
        da_o = da_o.reshape(b, s, DA_WIDTH).astype(x.dtype)

        x = x + jnp.concatenate([sb_o, cv_o, da_o], axis=-1) @ w_out[l]

        h = rms_norm(x, ffn_norm[l])
        i = l // 2
        if l % 2 == 0:
            x = x + swiglu(h, w_gate[i], w_up[i], w_down[i])
        else:
            x = x + moe_swiglu(h, w_router[i], e_gate[i], e_up[i], e_down[i])
    return rms_norm(x, final_norm)
```

```python
import functools
import math

import jax
import jax.numpy as jnp
from jax import lax
from jax.experimental import pallas as pl
from jax.experimental.pallas import tpu as pltpu

F32 = jnp.float32
BF16 = jnp.bfloat16

HEAD_DIM = 128
SB_HEADS = 6
DA_HEADS = 6
DA_HALF = HEAD_DIM // 2
CHUNK = 64
CONV_KERNEL = 31
CONV_HALO = 32
N_EXPERTS = 8
EPS = 1e-6
NEG_BIG = -1e30

VMEM_LIMIT = 56 * 1024 * 1024


def _params(sem, vmem=VMEM_LIMIT):
    return pltpu.CompilerParams(dimension_semantics=sem, vmem_limit_bytes=vmem)


def _rms(x, g):
    ms = jnp.mean(x * x, axis=-1, keepdims=True)
    return x * lax.rsqrt(ms + EPS) * g


def _dot(a, b):
    return jnp.dot(a, b, preferred_element_type=F32)


def _dot_nt(a, b):
    return lax.dot_general(a, b, (((1,), (1,)), ((), ())), preferred_element_type=F32)


def _norm_matmul_kernel(x_ref, g_ref, w_ref, cs_ref, o_ref, h_scr):
    @pl.when(pl.program_id(1) == 0)
    def _():
        h_scr[...] = _rms(x_ref[...], g_ref[...]).astype(BF16)

    o_ref[...] = (_dot(h_scr[...], w_ref[...]) * cs_ref[...]).astype(o_ref.dtype)


def _norm_in_proj(x2d, g, w, colscale, tm, tn):
    n, d = x2d.shape
    nout = w.shape[1]
    return pl.pallas_call(
        _norm_matmul_kernel,
        grid=(n // tm, nout // tn),
        in_specs=[
            pl.BlockSpec((tm, d), lambda i, j: (i, 0)),
            pl.BlockSpec((1, d), lambda i, j: (0, 0)),
            pl.BlockSpec((d, tn), lambda i, j: (0, j)),
            pl.BlockSpec((1, tn), lambda i, j: (0, j)),
        ],
        out_specs=pl.BlockSpec((tm, tn), lambda i, j: (i, j)),
        out_shape=jax.ShapeDtypeStruct((n, nout), BF16),
        scratch_shapes=[pltpu.VMEM((tm, d), BF16)],
        compiler_params=_params(("parallel", "arbitrary")),
        name="norm_in_proj",
    )(x2d, g, w, colscale)


def _sb_kernel(q_ref, k_ref, v_ref, o_ref, *, tq):
    qi = pl.program_id(2)
    q = q_ref[...]
    row = lax.broadcasted_iota(jnp.int32, (tq, tq), 0)
    col = lax.broadcasted_iota(jnp.int32, (tq, tq), 1)
    tri = jnp.where(row > col, 1.0, 0.0).astype(BF16)
    causal = col < row

    def block(kb, rem, acc, diag):
        start = pl.multiple_of(kb * tq, tq)
        k = k_ref[pl.ds(start, tq), :]
        v = v_ref[pl.ds(start, tq), :]
        z = _dot_nt(q, k)
        l1m = jnp.minimum(-z, 0.0) - jnp.log(1.0 + jnp.exp(-jnp.abs(z)))
        if diag:
            l1m = jnp.where(causal, l1m, 0.0)
        hi = l1m.astype(BF16)
        lo = (l1m - hi.astype(F32)).astype(BF16)
        after = rem + _dot(hi, tri) + _dot(lo, tri)
        w = jnp.exp(l1m + z + after)
        if diag:
            w = jnp.where(causal, w, 0.0)
        acc = acc + _dot(w.astype(BF16), v)
        rem = rem + jnp.sum(l1m, axis=1, keepdims=True)
        return rem, acc

    rem, acc = block(qi, jnp.zeros((tq, 1), F32), jnp.zeros((tq, HEAD_DIM), F32), True)

    def body(i, carry):
        return block(qi - i, carry[0], carry[1], False)

    rem, acc = lax.fori_loop(1, qi + 1, body, (rem, acc))
    o_ref[...] = acc.astype(o_ref.dtype)


def _sb_attention(proj, tq):
    b, s, _ = proj.shape
    q_off, k_off, v_off = 0, SB_HEADS, 2 * SB_HEADS
    return pl.pallas_call(
        functools.partial(_sb_kernel, tq=tq),
        grid=(b, SB_HEADS, s // tq),
        in_specs=[
            pl.BlockSpec((None, tq, HEAD_DIM), lambda bi, h, i: (bi, i, q_off + h)),
            pl.BlockSpec((None, s, HEAD_DIM), lambda bi, h, i: (bi, 0, k_off + h)),
            pl.BlockSpec((None, s, HEAD_DIM), lambda bi, h, i: (bi, 0, v_off + h)),
        ],
        out_specs=pl.BlockSpec((None, tq, HEAD_DIM), lambda bi, h, i: (bi, i, h)),
        out_shape=jax.ShapeDtypeStruct((b, s, SB_HEADS * HEAD_DIM), BF16),
        compiler_params=_params(("parallel", "parallel", "arbitrary")),
        name="sb_attention",
    )(proj, proj, proj)


def _da_kernel(slopes_ref, lam_ref, q_ref, k_ref, v_ref, g_ref, o_ref, *, tq, lam_init):
    h = pl.program_id(1)
    qi = pl.program_id(2)
    slope = slopes_ref[h]
    q = q_ref[...]
    lane = lax.broadcasted_iota(jnp.int32, (tq, HEAD_DIM), 1)
    zero = jnp.zeros_like(q)
    q1 = jnp.where(lane < DA_HALF, q, zero)
    q2 = jnp.where(lane >= DA_HALF, q, zero)
    row = lax.broadcasted_iota(jnp.int32, (tq, tq), 0)
    col = lax.broadcasted_iota(jnp.int32, (tq, tq), 1)
    delta = (row - col).astype(F32)

    def update(s, m, l, acc, v):
        m_new = jnp.maximum(m, jnp.max(s, axis=1, keepdims=True))
        alpha = jnp.exp(m - m_new)
        p = jnp.exp(s - m_new)
        l = alpha * l + jnp.sum(p, axis=1, keepdims=True)
        acc = alpha * acc + _dot(p.astype(BF16), v)
        return m_new, l, acc

    def block(kb, carry, diag):
        m1, l1, a1, m2, l2, a2 = carry
        start = pl.multiple_of(kb * tq, tq)
        k = k_ref[pl.ds(start, tq), :]
        v = v_ref[pl.ds(start, tq), :]
        s1 = _dot_nt(q1, k)
        s2 = _dot_nt(q2, k)
        if diag:
            allowed = (col // CHUNK) <= (row // CHUNK)
            bias = -slope * jnp.abs(delta)
            s1 = jnp.where(allowed, s1 + bias, NEG_BIG)
            s2 = jnp.where(allowed, s2 + bias, NEG_BIG)
        else:
            bias = -slope * (delta + ((qi - kb) * tq).astype(F32))
            s1 = s1 + bias
            s2 = s2 + bias
        m1, l1, a1 = update(s1, m1, l1, a1, v)
        m2, l2, a2 = update(s2, m2, l2, a2, v)
        return m1, l1, a1, m2, l2, a2

    col0 = jnp.full((tq, 1), NEG_BIG, F32)
    z1 = jnp.zeros((tq, 1), F32)
    za = jnp.zeros((tq, HEAD_DIM), F32)
    carry = block(qi, (col0, z1, za, col0, z1, za), True)
    carry = lax.fori_loop(1, qi + 1, lambda i, c: block(qi - i, c, False), carry)
    _, l1, a1, _, l2, a2 = carry

    lv = lam_ref[...]
    lam_full = (jnp.exp(jnp.sum(lv[0:1] * lv[1:2], axis=1, keepdims=True))
                - jnp.exp(jnp.sum(lv[2:3] * lv[3:4], axis=1, keepdims=True)) + lam_init)
    o = a1 / l1 - lam_full * (a2 / l2)
    o_ref[...] = (_rms(o, g_ref[...]) * (1.0 - lam_init)).astype(o_ref.dtype)


def _da_attention(proj, lam_l, diff_norm_l, slopes, lam_init, tq):
    b, s, width = proj.shape
    nblk = width // HEAD_DIM
    q_off, k_off, v_off = nblk - 3 * DA_HEADS, nblk - 2 * DA_HEADS, nblk - DA_HEADS
    return pl.pallas_call(
        functools.partial(_da_kernel, tq=tq, lam_init=lam_init),
        grid_spec=pltpu.PrefetchScalarGridSpec(
            num_scalar_prefetch=1,
            grid=(b, DA_HEADS, s // tq),
            in_specs=[
                pl.BlockSpec((4, DA_HALF), lambda bi, h, i, sl: (0, 0)),
                pl.BlockSpec((None, tq, HEAD_DIM), lambda bi, h, i, sl: (bi, i, q_off + h)),
                pl.BlockSpec((None, s, HEAD_DIM), lambda bi, h, i, sl: (bi, 0, k_off + h)),
                pl.BlockSpec((None, s, HEAD_DIM), lambda bi, h, i, sl: (bi, 0, v_off + h)),
                pl.BlockSpec((1, HEAD_DIM), lambda bi, h, i, sl: (0, h)),
            ],
            out_specs=pl.BlockSpec((None, tq, HEAD_DIM), lambda bi, h, i, sl: (bi, i, h)),
        ),
        out_shape=jax.ShapeDtypeStruct((b, s, DA_HEADS * HEAD_DIM), BF16),
        compiler_params=_params(("parallel", "parallel", "arbitrary")),
        name="da_attention",
    )(slopes, lam_l, proj, proj, proj, diff_norm_l.reshape(1, -1))


def _conv_kernel(a0, a1, g0, g1, ha0, ha1, hg0, hg1, cw_ref, cb_ref, lng_ref, lnb_ref, wpw_ref,
                 o_ref, scr0, scr1, *, ts, half):
    first = pl.program_id(1) == 0
    conv = []
    for c, (a, g, ha, hg, scr) in enumerate(((a0, g0, ha0, hg0, scr0), (a1, g1, ha1, hg1, scr1))):
        lo = c * half
        u = a[...].astype(F32) * jax.nn.sigmoid(g[...].astype(F32))
        hu = ha[...].astype(F32) * jax.nn.sigmoid(hg[...].astype(F32))
        scr[0:CONV_HALO, :] = jnp.where(first, 0.0, hu)
        scr[CONV_HALO:CONV_HALO + ts, :] = u
        acc = jnp.broadcast_to(cb_ref[:, lo:lo + half], (ts, half))
        for j in range(CONV_KERNEL):
            off = CONV_HALO - (CONV_KERNEL - 1) + j
            acc = acc + cw_ref[j:j + 1, lo:lo + half] * scr[off:off + ts, :]
        conv.append(acc)
    width = 2 * half
    mu = (jnp.sum(conv[0], axis=1, keepdims=True) + jnp.sum(conv[1], axis=1, keepdims=True)) / width
    d0 = conv[0] - mu
    d1 = conv[1] - mu
    var = (jnp.sum(d0 * d0, axis=1, keepdims=True) + jnp.sum(d1 * d1, axis=1, keepdims=True)) / width
    inv = lax.rsqrt(var + EPS)
    out = None
    for c, d in enumerate((d0, d1)):
        lo = c * half
        un = d * inv * lng_ref[:, lo:lo + half] + lnb_ref[:, lo:lo + half]
        act = (un * jax.nn.sigmoid(un)).astype(BF16)
        part = _dot(act, wpw_ref[lo:lo + half, :])
        out = part if out is None else out + part
    o_ref[...] = out.astype(o_ref.dtype)


def _conformer_conv(proj, conv_w, conv_b, ln_g, ln_b, w_pw, ts):
    b, s, _ = proj.shape
    width = conv_w.shape[1]
    half = width // 2
    a_blk = 3 * SB_HEADS * HEAD_DIM // half
    g_blk = a_blk + 2
    hpt = ts // CONV_HALO

    def cur(cb):
        return pl.BlockSpec((None, ts, half), lambda bi, i: (bi, i, cb))

    def halo(cb):
        return pl.BlockSpec((None, CONV_HALO, half),
                            lambda bi, i: (bi, jnp.maximum(i * hpt - 1, 0), cb))

    def full(shape):
        return pl.BlockSpec(shape, lambda bi, i: (0, 0))

    return pl.pallas_call(
        functools.partial(_conv_kernel, ts=ts, half=half),
        grid=(b, s // ts),
        in_specs=[cur(a_blk), cur(a_blk + 1), cur(g_blk), cur(g_blk + 1),
                  halo(a_blk), halo(a_blk + 1), halo(g_blk), halo(g_blk + 1),
                  full((CONV_KERNEL, width)), full((1, width)), full((1, width)), full((1, width)),
                  full((width, width))],
        out_specs=pl.BlockSpec((None, ts, width), lambda bi, i: (bi, i, 0)),
        out_shape=jax.ShapeDtypeStruct((b, s, width), BF16),
        scratch_shapes=[pltpu.VMEM((CONV_HALO + ts, half), F32)] * 2,
        compiler_params=_params(("parallel", "arbitrary")),
        name="conformer_conv",
    )(proj, proj, proj, proj, proj, proj, proj, proj,
      conv_w, conv_b.reshape(1, -1), ln_g.reshape(1, -1), ln_b.reshape(1, -1), w_pw)


def _out_proj_kernel(sb_ref, cv_ref, da_ref, x_ref, w_ref, g_ref, xo_ref, ho_ref, *, w_sb, w_cv):
    acc = _dot(sb_ref[...], w_ref[0:w_sb, :])
    acc = acc + _dot(cv_ref[...], w_ref[w_sb:w_sb + w_cv, :])
    acc = acc + _dot(da_ref[...], w_ref[w_sb + w_cv:, :])
    xn = x_ref[...] + acc
    xo_ref[...] = xn
    ho_ref[...] = _rms(xn, g_ref[...]).astype(ho_ref.dtype)


def _out_proj(sb_o, cv_o, da_o, x2d, w_out, g, h_dtype, tm):
    n, d = x2d.shape
    w_sb, w_cv, w_da = sb_o.shape[1], cv_o.shape[1], da_o.shape[1]
    row = lambda width: pl.BlockSpec((tm, width), lambda i: (i, 0))
    return pl.pallas_call(
        functools.partial(_out_proj_kernel, w_sb=w_sb, w_cv=w_cv),
        grid=(n // tm,),
        in_specs=[row(w_sb), row(w_cv), row(w_da), row(d),
                  pl.BlockSpec((d, d), lambda i: (0, 0)),
                  pl.BlockSpec((1, d), lambda i: (0, 0))],
        out_specs=[row(d), row(d)],
        out_shape=[jax.ShapeDtypeStruct((n, d), F32), jax.ShapeDtypeStruct((n, d), h_dtype)],
        compiler_params=_params(("parallel",)),
        name="out_proj",
    )(sb_o, cv_o, da_o, x2d, w_out, g)


def _ffn_kernel(te_ref, tv_ref, x_ref, *rest, residual):
    if residual:
        res_ref, wg_ref, wu_ref, wd_ref, o_ref = rest
    else:
        wg_ref, wu_ref, wd_ref, o_ref = rest
    i = pl.program_id(0)
    f = pl.program_id(1)
    valid = tv_ref[i] > 0

    @pl.when(valid)
    def _():
        x = x_ref[...].astype(BF16)
        g = _dot(x, wg_ref[...])
        u = _dot(x, wu_ref[...])
        a = (g * jax.nn.sigmoid(g) * u).astype(BF16)
        y = _dot(a, wd_ref[...])

        @pl.when(f == 0)
        def _():
            o_ref[...] = (res_ref[...] + y) if residual else y

        @pl.when(f > 0)
        def _():
            o_ref[...] += y

    @pl.when(jnp.logical_and(jnp.logical_not(valid), f == 0))
    def _():
        o_ref[...] = jnp.zeros_like(o_ref)


def _ffn(x_rows, res, wg, wu, wd, tile_expert, tile_valid, tm, tf):
    rows, d = x_rows.shape
    nf = wg.shape[2] // tf

    def fblk(f, i, tv):
        return jnp.where(tv[i] > 0, f, nf - 1)

    in_specs = [pl.BlockSpec((tm, d), lambda i, f, te, tv: (i, 0))]
    args = [x_rows]
    if res is not None:
        in_specs.append(pl.BlockSpec((tm, d), lambda i, f, te, tv: (i, 0)))
        args.append(res)
    in_specs += [
        pl.BlockSpec((None, d, tf), lambda i, f, te, tv: (te[i], 0, fblk(f, i, tv))),
        pl.BlockSpec((None, d, tf), lambda i, f, te, tv: (te[i], 0, fblk(f, i, tv))),
        pl.BlockSpec((None, tf, d), lambda i, f, te, tv: (te[i], fblk(f, i, tv), 0)),
    ]
    return pl.pallas_call(
        functools.partial(_ffn_kernel, residual=res is not None),
        grid_spec=pltpu.PrefetchScalarGridSpec(
            num_scalar_prefetch=2,
            grid=(rows // tm, nf),
            in_specs=in_specs,
            out_specs=pl.BlockSpec((tm, d), lambda i, f, te, tv: (i, 0)),
        ),
        out_shape=jax.ShapeDtypeStruct((rows, d), F32),
        compiler_params=_params(("parallel", "arbitrary")),
        name="swiglu_ffn",
    )(tile_expert, tile_valid, *args, wg, wu, wd)


def _router_kernel(h_ref, wr_ref, meta_ref, *, n_experts):
    logits = _dot_nt(wr_ref[...], h_ref[...].astype(BF16))
    e = lax.broadcasted_iota(jnp.int32, logits.shape, 0)
    logits = jnp.where(e < n_experts, logits, -jnp.inf)
    m1 = jnp.max(logits, axis=0, keepdims=True)
    i1 = jnp.min(jnp.where(logits == m1, e, n_experts), axis=0, keepdims=True)
    rest = jnp.where(e == i1, -jnp.inf, logits)
    m2 = jnp.max(rest, axis=0, keepdims=True)
    i2 = jnp.min(jnp.where(rest == m2, e, n_experts), axis=0, keepdims=True)
    t = jnp.exp(m2 - m1)
    g1 = 1.0 / (1.0 + t)
    g2 = t / (1.0 + t)
    r = lax.broadcasted_iota(jnp.int32, meta_ref.shape, 0)
    meta = jnp.where(r == 0, i1.astype(F32), 0.0)
    meta = jnp.where(r == 1, i2.astype(F32), meta)
    meta = jnp.where(r == 2, g1, meta)
    meta = jnp.where(r == 3, g2, meta)
    meta_ref[...] = meta


def _router(h, w_router, tm):
    n, d = h.shape
    n_experts = w_router.shape[1]
    wr = jnp.zeros((16, d), BF16).at[:n_experts].set(w_router.T.astype(BF16))
    return pl.pallas_call(
        functools.partial(_router_kernel, n_experts=n_experts),
        grid=(n // tm,),
        in_specs=[pl.BlockSpec((tm, d), lambda i: (i, 0)),
                  pl.BlockSpec((16, d), lambda i: (0, 0))],
        out_specs=pl.BlockSpec((8, tm), lambda i: (0, i)),
        out_shape=jax.ShapeDtypeStruct((8, n), F32),
        compiler_params=_params(("parallel",)),
        name="router",
    )(h, wr)


def _dispatch_kernel(pos0_ref, pos1_ref, h_hbm, xs_in, xs_out, sem, *, tm):
    del xs_in
    base = pl.program_id(0) * tm

    def copies(r):
        t = base + r
        src = h_hbm.at[pl.ds(t, 1)]
        return (pltpu.make_async_copy(src, xs_out.at[pl.ds(pos0_ref[t], 1)], sem),
                pltpu.make_async_copy(src, xs_out.at[pl.ds(pos1_ref[t], 1)], sem))

    def start(r, c):
        for cp in copies(r):
            cp.start()
        return c

    def wait(r, c):
        for cp in copies(r):
            cp.wait()
        return c

    lax.fori_loop(0, tm, start, 0)
    lax.fori_loop(0, tm, wait, 0)


def _dispatch(h, pos0, pos1, n_slots, tm):
    n, d = h.shape
    xs0 = jnp.zeros((n_slots, d), h.dtype)
    return pl.pallas_call(
        functools.partial(_dispatch_kernel, tm=tm),
        grid_spec=pltpu.PrefetchScalarGridSpec(
            num_scalar_prefetch=2,
            grid=(n // tm,),
            in_specs=[pl.BlockSpec(memory_space=pl.ANY), pl.BlockSpec(memory_space=pl.ANY)],
            out_specs=pl.BlockSpec(memory_space=pl.ANY),
            scratch_shapes=[pltpu.SemaphoreType.DMA(())],
        ),
        out_shape=jax.ShapeDtypeStruct((n_slots, d), h.dtype),
        input_output_aliases={3: 0},
        compiler_params=_params(("arbitrary",)),
        name="dispatch_rows",
    )(pos0, pos1, h, xs0)


def _combine_kernel(pos0_ref, pos1_ref, ys_hbm, x_ref, gate_ref, g_ref, o_ref, buf, sem, *, tm, final):
    base = pl.program_id(0) * tm

    def copies(r):
        t = base + r
        return (pltpu.make_async_copy(ys_hbm.at[pl.ds(pos0_ref[t], 1)], buf.at[0, pl.ds(r, 1)], sem),
                pltpu.make_async_copy(ys_hbm.at[pl.ds(pos1_ref[t], 1)], buf.at[1, pl.ds(r, 1)], sem))

    def start(r, c):
        for cp in copies(r):
            cp.start()
        return c

    def wait(r, c):
        for cp in copies(r):
            cp.wait()
        return c

    lax.fori_loop(0, tm, start, 0)
    lax.fori_loop(0, tm, wait, 0)
    gates = gate_ref[...]
    y = x_ref[...] + gates[:, 0:1] * buf[0] + gates[:, 1:2] * buf[1]
    o_ref[...] = _rms(y, g_ref[...]) if final else y


def _combine(ys, x2d, gates, pos0, pos1, g, final, tm):
    n, d = x2d.shape
    return pl.pallas_call(
        functools.partial(_combine_kernel, tm=tm, final=final),
        grid_spec=pltpu.PrefetchScalarGridSpec(
            num_scalar_prefetch=2,
            grid=(n // tm,),
            in_specs=[pl.BlockSpec(memory_space=pl.ANY),
                      pl.BlockSpec((tm, d), lambda i, p0, p1: (i, 0)),
                      pl.BlockSpec((tm, 2), lambda i, p0, p1: (i, 0)),
                      pl.BlockSpec((1, d), lambda i, p0, p1: (0, 0))],
            out_specs=pl.BlockSpec((tm, d), lambda i, p0, p1: (i, 0)),
            scratch_shapes=[pltpu.VMEM((2, tm, d), F32), pltpu.SemaphoreType.DMA(())],
        ),
        out_shape=jax.ShapeDtypeStruct((n, d), F32),
        compiler_params=_params(("arbitrary",)),
        name="combine_rows",
    )(pos0, pos1, ys, x2d, gates, g)


def _final_norm_kernel(x_ref, g_ref, o_ref):
    o_ref[...] = _rms(x_ref[...], g_ref[...])


def _final_norm(x2d, g, tm):
    n, d = x2d.shape
    return pl.pallas_call(
        _final_norm_kernel,
        grid=(n // tm,),
        in_specs=[pl.BlockSpec((tm, d), lambda i: (i, 0)), pl.BlockSpec((1, d), lambda i: (0, 0))],
        out_specs=pl.BlockSpec((tm, d), lambda i: (i, 0)),
        out_shape=jax.ShapeDtypeStruct((n, d), F32),
        compiler_params=_params(("parallel",)),
        name="final_norm",
    )(x2d, g)


def _routing_tables(meta, tm):
    n = meta.shape[1]
    experts = meta[0:2].astype(jnp.int32)
    gates = meta[2:4].T
    flat = experts.reshape(-1)
    onehot = (flat[:, None] == jnp.arange(N_EXPERTS)[None, :]).astype(jnp.int32)
    ranks = jnp.cumsum(onehot, axis=0) - onehot
    rank = jnp.sum(ranks * onehot, axis=1)
    counts = jnp.sum(onehot, axis=0)
    tiles = (counts + tm - 1) // tm
    tile_end = jnp.cumsum(tiles)
    tile_start = tile_end - tiles
    pos = (tile_start * tm)[flat] + rank
    n_tiles = (2 * n) // tm + N_EXPERTS
    tile_ids = jnp.arange(n_tiles)
    tile_valid = (tile_ids < tile_end[-1]).astype(jnp.int32)
    clipped = jnp.minimum(tile_ids, tile_end[-1] - 1)
    tile_expert = jnp.sum((clipped[:, None] >= tile_end[None, :]).astype(jnp.int32), axis=1)
    tile_expert = jnp.minimum(tile_expert, N_EXPERTS - 1).astype(jnp.int32)
    return pos[:n].astype(jnp.int32), pos[n:].astype(jnp.int32), gates, tile_expert, tile_valid


TM_IN, TN_IN = 1024, 1408
TQ_SB = 256
TQ_DA = 256
TS_CONV = 512
TM_OUT = 512
TM_FFN, TF_DENSE, TF_EXPERT = 512, 512, 256
TM_ROUTER = 512
TM_ROWS = 512


def kernel(x, attn_norm, w_in, w_out, lam, diff_norm, conv_w, conv_b, conv_ln_g, conv_ln_b, w_conv_out, ffn_norm, w_gate, w_up, w_down, w_router, e_gate, e_up, e_down, final_norm):
    b, s, d = x.shape
    n = b * s
    depth = w_in.shape[0]
    sb_w = SB_HEADS * HEAD_DIM
    da_w = DA_HEADS * HEAD_DIM
    cv_w = conv_w.shape[2]
    in_w = w_in.shape[2]

    colscale = jnp.ones((in_w,), F32)
    colscale = colscale.at[:sb_w].set(HEAD_DIM ** -0.5)
    da_q0 = in_w - 3 * da_w
    colscale = colscale.at[da_q0:da_q0 + da_w].set(DA_HALF ** -0.5).reshape(1, in_w)
    slopes = jnp.exp2(-8.0 * (jnp.arange(DA_HEADS, dtype=F32) + 1.0) / DA_HEADS)

    x2d = x.reshape(n, d)
    out = None
    for l in range(depth):
        lam_init = 0.8 - 0.6 * math.exp(-0.3 * l)
        moe = l % 2 == 1
        i = l // 2
        proj = _norm_in_proj(x2d, attn_norm[l].reshape(1, d), w_in[l].astype(BF16), colscale,
                             TM_IN, TN_IN).reshape(b, s, in_w)
        sb_o = _sb_attention(proj, TQ_SB).reshape(n, sb_w)
        cv_o = _conformer_conv(proj, conv_w[l], conv_b[l], conv_ln_g[l], conv_ln_b[l],
                               w_conv_out[l].astype(BF16), TS_CONV).reshape(n, cv_w)
        da_o = _da_attention(proj, lam[l], diff_norm[l], slopes, lam_init, TQ_DA).reshape(n, da_w)
        x2d, h = _out_proj(sb_o, cv_o, da_o, x2d, w_out[l].astype(BF16), ffn_norm[l].reshape(1, d),
                           F32 if moe else BF16, TM_OUT)
        if not moe:
            ones = jnp.ones((n // TM_FFN,), jnp.int32)
            x2d = _ffn(h, x2d, w_gate[i].astype(BF16)[None], w_up[i].astype(BF16)[None],
                       w_down[i].astype(BF16)[None], jnp.zeros_like(ones), ones, TM_FFN, TF_DENSE)
            if l == depth - 1:
                out = _final_norm(x2d, final_norm.reshape(1, d), TM_ROWS)
        else:
            meta = _router(h, w_router[i], TM_ROUTER)
            pos0, pos1, gates, tile_expert, tile_valid = _routing_tables(meta, TM_FFN)
            n_slots = 2 * n + N_EXPERTS * TM_FFN
            xs = _dispatch(h, pos0, pos1, n_slots, TM_ROWS)
            ys = _ffn(xs, None, e_gate[i].astype(BF16), e_up[i].astype(BF16), e_down[i].astype(BF16),
                      tile_expert, tile_valid, TM_FFN, TF_EXPERT)
            last = l == depth - 1
            x2d = _combine(ys, x2d, gates, pos0, pos1, final_norm.reshape(1, d), last, TM_ROWS)
            if last:
                out = x2d
    return out.reshape(b, s, d)
```

```python
import functools
import math

import jax
import jax.numpy as jnp
from jax import lax
from jax.experimental import pallas as pl
from jax.experimental.pallas import tpu as pltpu

F32 = jnp.float32
BF16 = jnp.bfloat16

HEAD_DIM = 128
SB_HEADS = 6
DA_HEADS = 6
DA_HALF = HEAD_DIM // 2
CHUNK = 64
CONV_KERNEL = 31
CONV_HALO = 32
N_EXPERTS = 8
EPS = 1e-6
NEG_BIG = -1e30
SB_EXIT = 104.0
LOG2E = 1.4426950408889634

VMEM_LIMIT = 56 * 1024 * 1024


def _params(sem, vmem=VMEM_LIMIT):
    return pltpu.CompilerParams(dimension_semantics=sem, vmem_limit_bytes=vmem)


def _rms(x, g):
    ms = jnp.mean(x * x, axis=-1, keepdims=True)
    return x * lax.rsqrt(ms + EPS) * g


def _dot(a, b):
    return jnp.dot(a, b, preferred_element_type=F32)


def _dot_nt(a, b):
    return lax.dot_general(a, b, (((1,), (1,)), ((), ())), preferred_element_type=F32)


def _norm_matmul_kernel(x_ref, g_ref, w_ref, cs_ref, o_ref, h_scr):
    @pl.when(pl.program_id(1) == 0)
    def _():
        h_scr[...] = _rms(x_ref[...], g_ref[...]).astype(BF16)

    o_ref[...] = (_dot(h_scr[...], w_ref[...]) * cs_ref[...]).astype(o_ref.dtype)


def _norm_in_proj(x2d, g, w, colscale, tm, tn):
    n, d = x2d.shape
    nout = w.shape[1]
    return pl.pallas_call(
        _norm_matmul_kernel,
        grid=(n // tm, nout // tn),
        in_specs=[
            pl.BlockSpec((tm, d), lambda i, j: (i, 0)),
            pl.BlockSpec((1, d), lambda i, j: (0, 0)),
            pl.BlockSpec((d, tn), lambda i, j: (0, j)),
            pl.BlockSpec((1, tn), lambda i, j: (0, j)),
        ],
        out_specs=pl.BlockSpec((tm, tn), lambda i, j: (i, j)),
        out_shape=jax.ShapeDtypeStruct((n, nout), BF16),
        scratch_shapes=[pltpu.VMEM((tm, d), BF16)],
        compiler_params=_params(("parallel", "arbitrary")),
        name="norm_in_proj",
    )(x2d, g, w, colscale)


def _sb_kernel(q_ref, k_ref, v_ref, o_ref, *, tq):
    qi = pl.program_id(2)
    q = q_ref[...]
    row = lax.broadcasted_iota(jnp.int32, (tq, tq), 0)
    col = lax.broadcasted_iota(jnp.int32, (tq, tq), 1)
    tri = jnp.where(row > col, 1.0, 0.0).astype(BF16)
    causal = col < row

    def block(kb, rem, acc, diag):
        start = pl.multiple_of(kb * tq, tq)
        k = k_ref[pl.ds(start, tq), :]
        v = v_ref[pl.ds(start, tq), :]
        z = _dot_nt(q, k)
        l1m = jnp.minimum(-z, 0.0) - jnp.log(1.0 + jnp.exp(-jnp.abs(z)))
        if diag:
            l1m = jnp.where(causal, l1m, 0.0)
        hi = l1m.astype(BF16)
        lo = (l1m - hi.astype(F32)).astype(BF16)
        after = rem + _dot(hi, tri) + _dot(lo, tri)
        w = jnp.exp(l1m + z + after)
        if diag:
            w = jnp.where(causal, w, 0.0)
        acc = acc + _dot(w.astype(BF16), v)
        rem = rem + jnp.sum(l1m, axis=1, keepdims=True)
        return rem, acc

    rem0 = jnp.zeros((tq, 1), F32)
    acc0 = jnp.zeros((tq, HEAD_DIM), F32)

    @pl.when(qi == 0)
    def _():
        _, acc = block(0, rem0, acc0, True)
        o_ref[...] = acc.astype(o_ref.dtype)

    @pl.when(qi > 0)
    def _():
        rem, acc = block(qi, rem0, acc0, True)
        rem, acc = block(qi - 1, rem, acc, False)

        def cond(c):
            return jnp.logical_and(c[0] >= 0, c[1] > -SB_EXIT)

        def body(c):
            r, a = block(c[0], c[2], c[3], False)
            return c[0] - 1, jnp.max(r), r, a

        _, _, _, acc = lax.while_loop(cond, body, (qi - 2, jnp.max(rem), rem, acc))
        o_ref[...] = acc.astype(o_ref.dtype)


def _sb_attention(proj, tq):
    b, s, _ = proj.shape
    q_off, k_off, v_off = 0, SB_HEADS, 2 * SB_HEADS
    return pl.pallas_call(
        functools.partial(_sb_kernel, tq=tq),
        grid=(b, SB_HEADS, s // tq),
        in_specs=[
            pl.BlockSpec((None, tq, HEAD_DIM), lambda bi, h, i: (bi, i, q_off + h)),
            pl.BlockSpec((None, s, HEAD_DIM), lambda bi, h, i: (bi, 0, k_off + h)),
            pl.BlockSpec((None, s, HEAD_DIM), lambda bi, h, i: (bi, 0, v_off + h)),
        ],
        out_specs=pl.BlockSpec((None, tq, HEAD_DIM), lambda bi, h, i: (bi, i, h)),
        out_shape=jax.ShapeDtypeStruct((b, s, SB_HEADS * HEAD_DIM), BF16),
        compiler_params=_params(("parallel", "parallel", "arbitrary")),
        name="sb_attention",
    )(proj, proj, proj)


def _da_kernel(slopes_ref, lam_ref, q_ref, k_ref, v_ref, g_ref, o_ref, vaug, *, tq, tk, lam_init):
    h = pl.program_id(1)
    qi = pl.program_id(2)
    nslope = -slopes_ref[h]

    @pl.when(qi == 0)
    def _():
        vaug[:, 0:HEAD_DIM] = v_ref[...]
        lane = lax.broadcasted_iota(jnp.int32, (vaug.shape[0], HEAD_DIM), 1)
        vaug[:, HEAD_DIM:] = jnp.where(lane == 0, 1.0, 0.0).astype(BF16)

    q = q_ref[...]
    lane = lax.broadcasted_iota(jnp.int32, (tq, HEAD_DIM), 1)
    zero = jnp.zeros_like(q)
    q1 = jnp.where(lane < DA_HALF, q, zero)
    q2 = jnp.where(lane >= DA_HALF, q, zero)
    ii = lax.broadcasted_iota(jnp.int32, (tq, tk), 0)
    jj = lax.broadcasted_iota(jnp.int32, (tq, tk), 1)
    dl = (ii - jj).astype(F32)
    kd = (qi * tq) // tk
    rel = qi * tq - kd * tk

    def update(st, c, m, acc, va):
        m_new = jnp.maximum(m, jnp.max(st, axis=1, keepdims=True) + c)
        alpha = jnp.exp2(m - m_new)
        p = jnp.exp2(st - (m_new - c))
        acc = alpha * acc + _dot(p.astype(BF16), va)
        return m_new, acc

    def block(kb, carry, diag):
        m1, a1, m2, a2 = carry
        start = pl.multiple_of(kb * tk, tk)
        k = k_ref[pl.ds(start, tk), :]
        va = vaug[pl.ds(start, tk), :]
        s1 = _dot_nt(q1, k)
        s2 = _dot_nt(q2, k)
        if diag:
            allowed = (jj // CHUNK) <= ((ii + rel) // CHUNK)
            bias = nslope * jnp.abs(dl + rel.astype(F32))
            s1 = jnp.where(allowed, s1 + bias, NEG_BIG)
            s2 = jnp.where(allowed, s2 + bias, NEG_BIG)
            c = jnp.float32(0.0)
        else:
            bias = nslope * dl
            s1 = s1 + bias
            s2 = s2 + bias
            c = nslope * (qi * tq - kb * tk).astype(F32)
        m1, a1 = update(s1, c, m1, a1, va)
        m2, a2 = update(s2, c, m2, a2, va)
        return m1, a1, m2, a2

    m0 = jnp.full((tq, 1), NEG_BIG, F32)
    a0 = jnp.zeros((tq, 2 * HEAD_DIM), F32)
    carry = block(kd, (m0, a0, m0, a0), True)
    carry = lax.fori_loop(0, kd, lambda i, c: block(kd - 1 - i, c, False), carry)
    _, a1, _, a2 = carry

    lv = lam_ref[...]
    lam_full = (jnp.exp(jnp.sum(lv[0:1] * lv[1:2], axis=1, keepdims=True))
                - jnp.exp(jnp.sum(lv[2:3] * lv[3:4], axis=1, keepdims=True)) + lam_init)
    o = (a1[:, :HEAD_DIM] / a1[:, HEAD_DIM:HEAD_DIM + 1]
         - lam_full * (a2[:, :HEAD_DIM] / a2[:, HEAD_DIM:HEAD_DIM + 1]))
    o_ref[...] = (_rms(o, g_ref[...]) * (1.0 - lam_init)).astype(o_ref.dtype)


def _da_attention(proj, lam_l, diff_norm_l, slopes, lam_init, tq, tk):
    b, s, width = proj.shape
    nblk = width // HEAD_DIM
    q_off, k_off, v_off = nblk - 3 * DA_HEADS, nblk - 2 * DA_HEADS, nblk - DA_HEADS
    return pl.pallas_call(
        functools.partial(_da_kernel, tq=tq, tk=tk, lam_init=lam_init),
        grid_spec=pltpu.PrefetchScalarGridSpec(
            num_scalar_prefetch=1,
            grid=(b, DA_HEADS, s // tq),
            in_specs=[
                pl.BlockSpec((4, DA_HALF), lambda bi, h, i, sl: (0, 0)),
                pl.BlockSpec((None, tq, HEAD_DIM), lambda bi, h, i, sl: (bi, i, q_off + h)),
                pl.BlockSpec((None, s, HEAD_DIM), lambda bi, h, i, sl: (bi, 0, k_off + h)),
                pl.BlockSpec((None, s, HEAD_DIM), lambda bi, h, i, sl: (bi, 0, v_off + h)),
                pl.BlockSpec((1, HEAD_DIM), lambda bi, h, i, sl: (0, h)),
            ],
            out_specs=pl.BlockSpec((None, tq, HEAD_DIM), lambda bi, h, i, sl: (bi, i, h)),
            scratch_shapes=[pltpu.VMEM((s, 2 * HEAD_DIM), BF16)],
        ),
        out_shape=jax.ShapeDtypeStruct((b, s, DA_HEADS * HEAD_DIM), BF16),
        compiler_params=_params(("parallel", "parallel", "arbitrary")),
        name="da_attention",
    )(slopes, lam_l, proj, proj, proj, diff_norm_l.reshape(1, -1))


def _conv_kernel(a0, a1, g0, g1, ha0, ha1, hg0, hg1, cw_ref, cb_ref, lng_ref, lnb_ref, wpw_ref,
                 o_ref, scr0, scr1, *, ts, half):
    first = pl.program_id(1) == 0
    conv = []
    for c, (a, g, ha, hg, scr) in enumerate(((a0, g0, ha0, hg0, scr0), (a1, g1, ha1, hg1, scr1))):
        lo = c * half
        u = a[...].astype(F32) * jax.nn.sigmoid(g[...].astype(F32))
        hu = ha[...].astype(F32) * jax.nn.sigmoid(hg[...].astype(F32))
        scr[0:CONV_HALO, :] = jnp.where(first, 0.0, hu)
        scr[CONV_HALO:CONV_HALO + ts, :] = u
        acc = jnp.broadcast_to(cb_ref[:, lo:lo + half], (ts, half))
        for j in range(CONV_KERNEL):
            off = CONV_HALO - (CONV_KERNEL - 1) + j
            acc = acc + cw_ref[j:j + 1, lo:lo + half] * scr[off:off + ts, :]
        conv.append(acc)
    width = 2 * half
    mu = (jnp.sum(conv[0], axis=1, keepdims=True) + jnp.sum(conv[1], axis=1, keepdims=True)) / width
    d0 = conv[0] - mu
    d1 = conv[1] - mu
    var = (jnp.sum(d0 * d0, axis=1, keepdims=True) + jnp.sum(d1 * d1, axis=1, keepdims=True)) / width
    inv = lax.rsqrt(var + EPS)
    out = None
    for c, d in enumerate((d0, d1)):
        lo = c * half
        un = d * inv * lng_ref[:, lo:lo + half] + lnb_ref[:, lo:lo + half]
        act = (un * jax.nn.sigmoid(un)).astype(BF16)
        part = _dot(act, wpw_ref[lo:lo + half, :])
        out = part if out is None else out + part
    o_ref[...] = out.astype(o_ref.dtype)


def _conformer_conv(proj, conv_w, conv_b, ln_g, ln_b, w_pw, ts):
    b, s, _ = proj.shape
    width = conv_w.shape[1]
    half = width // 2
    a_blk = 3 * SB_HEADS * HEAD_DIM // half
    g_blk = a_blk + 2
    hpt = ts // CONV_HALO

    def cur(cb):
        return pl.BlockSpec((None, ts, half), lambda bi, i: (bi, i, cb))

    def halo(cb):
        return pl.BlockSpec((None, CONV_HALO, half),
                            lambda bi, i: (bi, jnp.maximum(i * hpt - 1, 0), cb))

    def full(shape):
        return pl.BlockSpec(shape, lambda bi, i: (0, 0))

    return pl.pallas_call(
        functools.partial(_conv_kernel, ts=ts, half=half),
        grid=(b, s // ts),
        in_specs=[cur(a_blk), cur(a_blk + 1), cur(g_blk), cur(g_blk + 1),
                  halo(a_blk), halo(a_blk + 1), halo(g_blk), halo(g_blk + 1),
                  full((CONV_KERNEL, width)), full((1, width)), full((1, width)), full((1, width)),
                  full((width, width))],
        out_specs=pl.BlockSpec((None, ts, width), lambda bi, i: (bi, i, 0)),
        out_shape=jax.ShapeDtypeStruct((b, s, width), BF16),
        scratch_shapes=[pltpu.VMEM((CONV_HALO + ts, half), F32)] * 2,
        compiler_params=_params(("parallel", "arbitrary")),
        name="conformer_conv",
    )(proj, proj, proj, proj, proj, proj, proj, proj,
      conv_w, conv_b.reshape(1, -1), ln_g.reshape(1, -1), ln_b.reshape(1, -1), w_pw)


def _out_proj_kernel(sb_ref, cv_ref, da_ref, x_ref, w_ref, g_ref, xo_ref, ho_ref, *, w_sb, w_cv):
    acc = _dot(sb_ref[...], w_ref[0:w_sb, :])
    acc = acc + _dot(cv_ref[...], w_ref[w_sb:w_sb + w_cv, :])
    acc = acc + _dot(da_ref[...], w_ref[w_sb + w_cv:, :])
    xn = x_ref[...] + acc
    xo_ref[...] = xn
    ho_ref[...] = _rms(xn, g_ref[...]).astype(ho_ref.dtype)


def _out_proj(sb_o, cv_o, da_o, x2d, w_out, g, h_dtype, tm):
    n, d = x2d.shape
    w_sb, w_cv, w_da = sb_o.shape[1], cv_o.shape[1], da_o.shape[1]
    row = lambda width: pl.BlockSpec((tm, width), lambda i: (i, 0))
    return pl.pallas_call(
        functools.partial(_out_proj_kernel, w_sb=w_sb, w_cv=w_cv),
        grid=(n // tm,),
        in_specs=[row(w_sb), row(w_cv), row(w_da), row(d),
                  pl.BlockSpec((d, d), lambda i: (0, 0)),
                  pl.BlockSpec((1, d), lambda i: (0, 0))],
        out_specs=[row(d), row(d)],
        out_shape=[jax.ShapeDtypeStruct((n, d), F32), jax.ShapeDtypeStruct((n, d), h_dtype)],
        compiler_params=_params(("parallel",)),
        name="out_proj",
    )(sb_o, cv_o, da_o, x2d, w_out, g)


def _ffn_kernel(te_ref, tv_ref, x_ref, *rest, residual):
    if residual:
        res_ref, wg_ref, wu_ref, wd_ref, o_ref, xb_scr = rest
    else:
        wg_ref, wu_ref, wd_ref, o_ref, xb_scr = rest
    i = pl.program_id(0)
    f = pl.program_id(1)

    @pl.when(f == 0)
    def _():
        xb_scr[...] = x_ref[...].astype(BF16)
        o_ref[...] = res_ref[...] if residual else jnp.zeros_like(o_ref)

    @pl.when(tv_ref[i] > 0)
    def _():
        x = xb_scr[...]
        g = _dot(x, wg_ref[...])
        u = _dot(x, wu_ref[...])
        a = (g * jax.nn.sigmoid(g) * u).astype(BF16)
        o_ref[...] += _dot(a, wd_ref[...])


def _ffn(x_rows, res, wg, wu, wd, tile_expert, tile_valid, tm, tf):
    rows, d = x_rows.shape
    nf = wg.shape[2] // tf

    def fblk(f, i, tv):
        return jnp.where(tv[i] > 0, f, nf - 1)

    in_specs = [pl.BlockSpec((tm, d), lambda i, f, te, tv: (i, 0))]
    args = [x_rows]
    if res is not None:
        in_specs.append(pl.BlockSpec((tm, d), lambda i, f, te, tv: (i, 0)))
        args.append(res)
    in_specs += [
        pl.BlockSpec((None, d, tf), lambda i, f, te, tv: (te[i], 0, fblk(f, i, tv))),
        pl.BlockSpec((None, d, tf), lambda i, f, te, tv: (te[i], 0, fblk(f, i, tv))),
        pl.BlockSpec((None, tf, d), lambda i, f, te, tv: (te[i], fblk(f, i, tv), 0)),
    ]
    return pl.pallas_call(
        functools.partial(_ffn_kernel, residual=res is not None),
        grid_spec=pltpu.PrefetchScalarGridSpec(
            num_scalar_prefetch=2,
            grid=(rows // tm, nf),
            in_specs=in_specs,
            out_specs=pl.BlockSpec((tm, d), lambda i, f, te, tv: (i, 0)),
            scratch_shapes=[pltpu.VMEM((tm, d), BF16)],
        ),
        out_shape=jax.ShapeDtypeStruct((rows, d), F32),
        compiler_params=_params(("parallel", "arbitrary")),
        name="swiglu_ffn",
    )(tile_expert, tile_valid, *args, wg, wu, wd)


def _router_kernel(h_ref, wr_ref, meta_ref, *, n_experts):
    logits = _dot_nt(wr_ref[...], h_ref[...].astype(BF16))
    e = lax.broadcasted_iota(jnp.int32, logits.shape, 0)
    logits = jnp.where(e < n_experts, logits, -jnp.inf)
    m1 = jnp.max(logits, axis=0, keepdims=True)
    i1 = jnp.min(jnp.where(logits == m1, e, n_experts), axis=0, keepdims=True)
    rest = jnp.where(e == i1, -jnp.inf, logits)
    m2 = jnp.max(rest, axis=0, keepdims=True)
    i2 = jnp.min(jnp.where(rest == m2, e, n_experts), axis=0, keepdims=True)
    t = jnp.exp(m2 - m1)
    g1 = 1.0 / (1.0 + t)
    g2 = t / (1.0 + t)
    r = lax.broadcasted_iota(jnp.int32, meta_ref.shape, 0)
    meta = jnp.where(r == 0, i1.astype(F32), 0.0)
    meta = jnp.where(r == 1, i2.astype(F32), meta)
    meta = jnp.where(r == 2, g1, meta)
    meta = jnp.where(r == 3, g2, meta)
    meta_ref[...] = meta


def _router(h, w_router, tm):
    n, d = h.shape
    n_experts = w_router.shape[1]
    wr = jnp.zeros((16, d), BF16).at[:n_experts].set(w_router.T.astype(BF16))
    return pl.pallas_call(
        functools.partial(_router_kernel, n_experts=n_experts),
        grid=(n // tm,),
        in_specs=[pl.BlockSpec((tm, d), lambda i: (i, 0)),
                  pl.BlockSpec((16, d), lambda i: (0, 0))],
        out_specs=pl.BlockSpec((8, tm), lambda i: (0, i)),
        out_shape=jax.ShapeDtypeStruct((8, n), F32),
        compiler_params=_params(("parallel",)),
        name="router",
    )(h, wr)


def _dispatch_kernel(pos0_ref, pos1_ref, h_ref, xs_in, xs_out, sem, *, tm):
    del xs_in
    base = pl.program_id(0) * tm

    def copies(r):
        t = base + r
        src = h_ref.at[pl.ds(r, 1)]
        return (pltpu.make_async_copy(src, xs_out.at[pl.ds(pos0_ref[t], 1)], sem),
                pltpu.make_async_copy(src, xs_out.at[pl.ds(pos1_ref[t], 1)], sem))

    def start(r, c):
        for cp in copies(r):
            cp.start()
        return c

    def wait(r, c):
        for cp in copies(r):
            cp.wait()
        return c

    lax.fori_loop(0, tm, start, 0)
    lax.fori_loop(0, tm, wait, 0)


def _dispatch(h, pos0, pos1, n_slots, tm):
    n, d = h.shape
    xs0 = jnp.zeros((n_slots, d), h.dtype)
    return pl.pallas_call(
        functools.partial(_dispatch_kernel, tm=tm),
        grid_spec=pltpu.PrefetchScalarGridSpec(
            num_scalar_prefetch=2,
            grid=(n // tm,),
            in_specs=[pl.BlockSpec((tm, d), lambda i, p0, p1: (i, 0)),
                      pl.BlockSpec(memory_space=pl.ANY)],
            out_specs=pl.BlockSpec(memory_space=pl.ANY),
            scratch_shapes=[pltpu.SemaphoreType.DMA(())],
        ),
        out_shape=jax.ShapeDtypeStruct((n_slots, d), h.dtype),
        input_output_aliases={3: 0},
        compiler_params=_params(("arbitrary",)),
        name="dispatch_rows",
    )(pos0, pos1, h, xs0)


def _combine_kernel(pos0_ref, pos1_ref, ys_hbm, x_ref, gate_ref, g_ref, o_ref, buf, sem, *, tm, final):
    base = pl.program_id(0) * tm

    def copies(r):
        t = base + r
        return (pltpu.make_async_copy(ys_hbm.at[pl.ds(pos0_ref[t], 1)], buf.at[0, pl.ds(r, 1)], sem),
                pltpu.make_async_copy(ys_hbm.at[pl.ds(pos1_ref[t], 1)], buf.at[1, pl.ds(r, 1)], sem))

    def start(r, c):
        for cp in copies(r):
            cp.start()
        return c

    def wait(r, c):
        for cp in copies(r):
            cp.wait()
        return c

    lax.fori_loop(0, tm, start, 0)
    lax.fori_loop(0, tm, wait, 0)
    gates = gate_ref[...]
    y = x_ref[...] + gates[:, 0:1] * buf[0] + gates[:, 1:2] * buf[1]
    o_ref[...] = _rms(y, g_ref[...]) if final else y


def _combine(ys, x2d, gates, pos0, pos1, g, final, tm):
    n, d = x2d.shape
    return pl.pallas_call(
        functools.partial(_combine_kernel, tm=tm, final=final),
        grid_spec=pltpu.PrefetchScalarGridSpec(
            num_scalar_prefetch=2,
            grid=(n // tm,),
            in_specs=[pl.BlockSpec(memory_space=pl.ANY),
                      pl.BlockSpec((tm, d), lambda i, p0, p1: (i, 0)),
                      pl.BlockSpec((tm, 2), lambda i, p0, p1: (i, 0)),
                      pl.BlockSpec((1, d), lambda i, p0, p1: (0, 0))],
            out_specs=pl.BlockSpec((tm, d), lambda i, p0, p1: (i, 0)),
            scratch_shapes=[pltpu.VMEM((2, tm, d), F32), pltpu.SemaphoreType.DMA(())],
        ),
        out_shape=jax.ShapeDtypeStruct((n, d), F32),
        compiler_params=_params(("arbitrary",)),
        name="combine_rows",
    )(pos0, pos1, ys, x2d, gates, g)


def _final_norm_kernel(x_ref, g_ref, o_ref):
    o_ref[...] = _rms(x_ref[...], g_ref[...])


def _final_norm(x2d, g, tm):
    n, d = x2d.shape
    return pl.pallas_call(
        _final_norm_kernel,
        grid=(n // tm,),
        in_specs=[pl.BlockSpec((tm, d), lambda i: (i, 0)), pl.BlockSpec((1, d), lambda i: (0, 0))],
        out_specs=pl.BlockSpec((tm, d), lambda i: (i, 0)),
        out_shape=jax.ShapeDtypeStruct((n, d), F32),
        compiler_params=_params(("parallel",)),
        name="final_norm",
    )(x2d, g)


def _routing_tables(meta, tm):
    n = meta.shape[1]
    experts = meta[0:2].astype(jnp.int32)
    gates = meta[2:4].T
    flat = experts.reshape(-1)
    onehot = (flat[:, None] == jnp.arange(N_EXPERTS)[None, :]).astype(jnp.int32)
    ranks = jnp.cumsum(onehot, axis=0) - onehot
    rank = jnp.sum(ranks * onehot, axis=1)
    counts = jnp.sum(onehot, axis=0)
    tiles = (counts + tm - 1) // tm
    tile_end = jnp.cumsum(tiles)
    tile_start = tile_end - tiles
    pos = (tile_start * tm)[flat] + rank
    n_tiles = (2 * n) // tm + N_EXPERTS
    tile_ids = jnp.arange(n_tiles)
    tile_valid = (tile_ids < tile_end[-1]).astype(jnp.int32)
    clipped = jnp.minimum(tile_ids, tile_end[-1] - 1)
    tile_expert = jnp.sum((clipped[:, None] >= tile_end[None, :]).astype(jnp.int32), axis=1)
    tile_expert = jnp.minimum(tile_expert, N_EXPERTS - 1).astype(jnp.int32)
    return pos[:n].astype(jnp.int32), pos[n:].astype(jnp.int32), gates, tile_expert, tile_valid


TM_IN, TN_IN = 1024, 1408
TQ_SB = 256
TQ_DA, TK_DA = 256, 512
TS_CONV = 512
TM_OUT = 512
TM_FFN, TF_DENSE, TF_EXPERT = 512, 512, 256
TM_ROUTER = 512
TM_ROWS = 512


def kernel(x, attn_norm, w_in, w_out, lam, diff_norm, conv_w, conv_b, conv_ln_g, conv_ln_b, w_conv_out, ffn_norm, w_gate, w_up, w_down, w_router, e_gate, e_up, e_down, final_norm):
    b, s, d = x.shape
    n = b * s
    depth = w_in.shape[0]
    sb_w = SB_HEADS * HEAD_DIM
    da_w = DA_HEADS * HEAD_DIM
    cv_w = conv_w.shape[2]
    in_w = w_in.shape[2]

    colscale = jnp.ones((in_w,), F32)
    colscale = colscale.at[:sb_w].set(HEAD_DIM ** -0.5)
    da_q0 = in_w - 3 * da_w
    colscale = colscale.at[da_q0:da_q0 + da_w].set(DA_HALF ** -0.5 * LOG2E).reshape(1, in_w)
    slopes = jnp.exp2(-8.0 * (jnp.arange(DA_HEADS, dtype=F32) + 1.0) / DA_HEADS) * LOG2E

    x2d = x.reshape(n, d)
    out = None
    for l in range(depth):
        lam_init = 0.8 - 0.6 * math.exp(-0.3 * l)
        moe = l % 2 == 1
        i = l // 2
        proj = _norm_in_proj(x2d, attn_norm[l].reshape(1, d), w_in[l].astype(BF16), colscale,
                             TM_IN, TN_IN).reshape(b, s, in_w)
        sb_o = _sb_attention(proj, TQ_SB).reshape(n, sb_w)
        cv_o = _conformer_conv(proj, conv_w[l], conv_b[l], conv_ln_g[l], conv_ln_b[l],
                               w_conv_out[l].astype(BF16), TS_CONV).reshape(n, cv_w)
        da_o = _da_attention(proj, lam[l], diff_norm[l], slopes, lam_init, TQ_DA, TK_DA).reshape(n, da_w)
        x2d, h = _out_proj(sb_o, cv_o, da_o, x2d, w_out[l].astype(BF16), ffn_norm[l].reshape(1, d),
                           F32 if moe else BF16, TM_OUT)
        if not moe:
            ones = jnp.ones((n // TM_FFN,), jnp.int32)
            x2d = _ffn(h, x2d, w_gate[i].astype(BF16)[None], w_up[i].astype(BF16)[None],
                       w_down[i].astype(BF16)[None], jnp.zeros_like(ones), ones, TM_FFN, TF_DENSE)
            if l == depth - 1:
                out = _final_norm(x2d, final_norm.reshape(1, d), TM_ROWS)
        else:
            meta = _router(h, w_router[i], TM_ROUTER)
            pos0, pos1, gates, tile_expert, tile_valid = _routing_tables(meta, TM_FFN)
            n_slots = 2 * n + N_EXPERTS * TM_FFN
            xs = _dispatch(h, pos0, pos1, n_slots, TM_ROWS)
            ys = _ffn(xs, None, e_gate[i].astype(BF16), e_up[i].astype(BF16), e_down[i].astype(BF16),
                      tile_expert, tile_valid, TM_FFN, TF_EXPERT)
            last = l == depth - 1
            x2d = _combine(ys, x2d, gates, pos0, pos1, final_norm.reshape(1, d), last, TM_ROWS)
            if last:
                out = x2d
    return out.reshape(b, s, d)
```

```python
import functools
import math

import jax
import jax.numpy as jnp
from jax import lax
from jax.experimental import pallas as pl
from jax.experimental.pallas import tpu as pltpu

F32 = jnp.float32
BF16 = jnp.bfloat16

HEAD_DIM = 128
SB_HEADS = 6
DA_HEADS = 6
DA_HALF = HEAD_DIM // 2
CHUNK = 64
CONV_KERNEL = 31
CONV_HALO = 32
N_EXPERTS = 8
EPS = 1e-6
NEG_BIG = -1e30
SB_EXIT = 104.0
LOG2E = 1.4426950408889634
DA_SUM_ROWS = 16

VMEM_LIMIT = 56 * 1024 * 1024


def _params(sem, vmem=VMEM_LIMIT):
    return pltpu.CompilerParams(dimension_semantics=sem, vmem_limit_bytes=vmem)


def _rms(x, g):
    ms = jnp.mean(x * x, axis=-1, keepdims=True)
    return x * lax.rsqrt(ms + EPS) * g


def _dot(a, b):
    return jnp.dot(a, b, preferred_element_type=F32)


def _dot_nt(a, b):
    return lax.dot_general(a, b, (((1,), (1,)), ((), ())), preferred_element_type=F32)


def _norm_matmul_kernel(x_ref, g_ref, w_ref, cs_ref, o_ref, h_scr):
    @pl.when(pl.program_id(1) == 0)
    def _():
        h_scr[...] = _rms(x_ref[...], g_ref[...]).astype(BF16)

    o_ref[...] = (_dot(h_scr[...], w_ref[...]) * cs_ref[...]).astype(o_ref.dtype)


def _norm_in_proj(x2d, g, w, colscale, tm, tn):
    n, d = x2d.shape
    nout = w.shape[1]
    return pl.pallas_call(
        _norm_matmul_kernel,
        grid=(n // tm, nout // tn),
        in_specs=[
            pl.BlockSpec((tm, d), lambda i, j: (i, 0)),
            pl.BlockSpec((1, d), lambda i, j: (0, 0)),
            pl.BlockSpec((d, tn), lambda i, j: (0, j)),
            pl.BlockSpec((1, tn), lambda i, j: (0, j)),
        ],
        out_specs=pl.BlockSpec((tm, tn), lambda i, j: (i, j)),
        out_shape=jax.ShapeDtypeStruct((n, nout), BF16),
        scratch_shapes=[pltpu.VMEM((tm, d), BF16)],
        compiler_params=_params(("parallel", "arbitrary")),
        name="norm_in_proj",
    )(x2d, g, w, colscale)


def _sb_kernel(q_ref, k_ref, v_ref, o_ref, *, tq):
    qi = pl.program_id(2)
    q = q_ref[...]
    row = lax.broadcasted_iota(jnp.int32, (tq, tq), 0)
    col = lax.broadcasted_iota(jnp.int32, (tq, tq), 1)
    tri = jnp.where(row > col, 1.0, 0.0).astype(BF16)
    causal = col < row

    def log1m_sigmoid(z):
        return jnp.minimum(-z, 0.0) - jnp.log(1.0 + jnp.exp(-jnp.abs(z)))

    def later_sum(l1m):
        hi = l1m.astype(BF16)
        lo = (l1m - hi.astype(F32)).astype(BF16)
        return _dot(hi, tri) + _dot(lo, tri)

    def block(kb, rem, acc, diag):
        start = pl.multiple_of(kb * tq, tq)
        k = k_ref[pl.ds(start, tq), :]
        v = v_ref[pl.ds(start, tq), :]
        z = _dot_nt(q, k)
        l1m = log1m_sigmoid(z)
        if diag:
            l1m = jnp.where(causal, l1m, 0.0)
        after = rem + later_sum(l1m)
        w = jnp.exp(l1m + z + after)
        if diag:
            w = jnp.where(causal, w, 0.0)
        acc = acc + _dot(w.astype(BF16), v)
        rem = rem + jnp.sum(l1m, axis=1, keepdims=True)
        return rem, acc

    rem0 = jnp.zeros((tq, 1), F32)
    acc0 = jnp.zeros((tq, HEAD_DIM), F32)

    @pl.when(qi == 0)
    def _():
        _, acc = block(0, rem0, acc0, True)
        o_ref[...] = acc.astype(o_ref.dtype)

    def own_and_previous_block():
        start = pl.multiple_of((qi - 1) * tq, tq)
        z = _dot_nt(q, k_ref[pl.ds(start, 2 * tq), :])
        l1m = log1m_sigmoid(z)
        l_prev = l1m[:, :tq]
        l_own = jnp.where(causal, l1m[:, tq:], 0.0)
        sum_own = jnp.sum(l_own, axis=1, keepdims=True)
        w_own = jnp.where(causal, jnp.exp(l_own + z[:, tq:] + later_sum(l_own)), 0.0)
        w_prev = jnp.exp(l_prev + z[:, :tq] + (later_sum(l_prev) + sum_own))
        w = jnp.concatenate([w_prev, w_own], axis=1).astype(BF16)
        acc = _dot(w, v_ref[pl.ds(start, 2 * tq), :])
        return sum_own + jnp.sum(l_prev, axis=1, keepdims=True), acc

    @pl.when(qi > 0)
    def _():
        rem, acc = own_and_previous_block()

        def cond(c):
            return jnp.logical_and(c[0] >= 0, c[1] > -SB_EXIT)

        def body(c):
            r, a = block(c[0], c[2], c[3], False)
            return c[0] - 1, jnp.max(r), r, a

        _, _, _, acc = lax.while_loop(cond, body, (qi - 2, jnp.max(rem), rem, acc))
        o_ref[...] = acc.astype(o_ref.dtype)


def _sb_attention(proj, tq):
    b, s, _ = proj.shape
    q_off, k_off, v_off = 0, SB_HEADS, 2 * SB_HEADS
    return pl.pallas_call(
        functools.partial(_sb_kernel, tq=tq),
        grid=(b, SB_HEADS, s // tq),
        in_specs=[
            pl.BlockSpec((None, tq, HEAD_DIM), lambda bi, h, i: (bi, i, q_off + h)),
            pl.BlockSpec((None, s, HEAD_DIM), lambda bi, h, i: (bi, 0, k_off + h)),
            pl.BlockSpec((None, s, HEAD_DIM), lambda bi, h, i: (bi, 0, v_off + h)),
        ],
        out_specs=pl.BlockSpec((None, tq, HEAD_DIM), lambda bi, h, i: (bi, i, h)),
        out_shape=jax.ShapeDtypeStruct((b, s, SB_HEADS * HEAD_DIM), BF16),
        compiler_params=_params(("parallel", "parallel", "arbitrary")),
        name="sb_attention",
    )(proj, proj, proj)


def _da_position_terms(pos, ns, key_side):
    axis = 1 if key_side else 0
    idx = lax.broadcasted_iota(jnp.int32, pos.shape, axis)
    nh = ns.astype(BF16).astype(F32)
    nl = (ns - nh).astype(BF16).astype(F32)
    hi = (pos // CHUNK).astype(F32)
    lo = (pos % CHUNK).astype(F32)
    sign = 1.0 if key_side else -1.0
    coef = jnp.where(idx % 2 == 0, nh, nl) * jnp.where(idx % 4 < 2, sign * CHUNK, sign)
    posv = jnp.where(idx % 4 < 2, hi, lo)
    first = (idx < 4) if key_side else (idx >= 4)
    return jnp.where(idx < 8, jnp.where(first, coef, posv), 0.0).astype(BF16)


def _da_kernel(slopes_ref, lam_ref, q_ref, k_ref, v_ref, g_ref, o_ref, kaug, vaug, sa, sb, *, tq, tk,
               lam_init):
    h = pl.program_id(1)
    qi = pl.program_id(2)
    nslope = -slopes_ref[h]
    nkb, rows, _ = vaug.shape

    @pl.when(qi == 0)
    def _():
        sub = lax.broadcasted_iota(jnp.int32, (rows - HEAD_DIM, tk), 0)
        tail = jnp.where(sub == 0, 1.0, 0.0).astype(BF16)
        ns = jnp.full((tk, HEAD_DIM), nslope, F32)
        for c in range(nkb):
            vaug[c, 0:HEAD_DIM, :] = v_ref[c * tk:(c + 1) * tk, :].astype(F32).T.astype(BF16)
            vaug[c, HEAD_DIM:rows, :] = tail
            kaug[c * tk:(c + 1) * tk, 0:HEAD_DIM] = k_ref[c * tk:(c + 1) * tk, :]
            pos = c * tk + lax.broadcasted_iota(jnp.int32, (tk, HEAD_DIM), 0)
            kaug[c * tk:(c + 1) * tk, HEAD_DIM:] = _da_position_terms(pos, ns, True)

    qt = q_ref[...].astype(F32).T
    sub = lax.broadcasted_iota(jnp.int32, (HEAD_DIM, tq), 0)
    tpos = qi * tq + lax.broadcasted_iota(jnp.int32, (HEAD_DIM, tq), 1)
    qpos = _da_position_terms(tpos, jnp.full((HEAD_DIM, tq), nslope, F32), False)
    q1 = jnp.concatenate([jnp.where(sub < DA_HALF, qt, 0.0).astype(BF16), qpos], axis=0)
    q2 = jnp.concatenate([jnp.where(sub >= DA_HALF, qt, 0.0).astype(BF16), qpos], axis=0)
    kd = (qi * tq) // tk
    rel = qi * tq - kd * tk

    def update(st, m, acc, vt):
        m_new = jnp.maximum(m, jnp.max(st, axis=0, keepdims=True))
        alpha = jnp.exp2(m - m_new)
        p = jnp.exp2(st - m_new).astype(BF16)
        acc = alpha * acc + _dot(vt, p)
        return m_new, acc

    def put_scores(s_ref, kb):
        k = kaug[pl.ds(pl.multiple_of(kb * tk, tk), tk), :]
        s_ref[0] = _dot(k, q1)
        s_ref[1] = _dot(k, q2)

    def block(kb, s_ref, carry, diag):
        m1, a1, m2, a2 = carry
        vt = vaug[kb]
        s1 = s_ref[0]
        s2 = s_ref[1]
        if diag:
            jj = lax.broadcasted_iota(jnp.int32, (tk, tq), 0)
            ii = lax.broadcasted_iota(jnp.int32, (tk, tq), 1)
            allowed = (jj // CHUNK) <= ((ii + rel) // CHUNK)
            d = (ii + rel - jj).astype(F32)
            fix = jnp.where(d < 0.0, -2.0 * nslope * d, 0.0)
            s1 = jnp.where(allowed, s1 + fix, NEG_BIG)
            s2 = jnp.where(allowed, s2 + fix, NEG_BIG)
        m1, a1 = update(s1, m1, a1, vt)
        m2, a2 = update(s2, m2, a2, vt)
        return m1, a1, m2, a2

    m0 = jnp.full((1, tq), NEG_BIG, F32)
    a0 = jnp.zeros((rows, tq), F32)
    put_scores(sa, kd)
    put_scores(sb, jnp.maximum(kd - 1, 0))
    carry = block(kd, sa, (m0, a0, m0, a0), True)

    def odd_block(c):
        c = block(kd - 1, sb, c, False)
        put_scores(sb, jnp.maximum(kd - 2, 0))
        return c

    carry = lax.cond(kd % 2 == 1, odd_block, lambda c: c, carry)
    nb = kd - 1 - kd % 2

    def pair(j, c):
        kb = nb - 2 * j
        put_scores(sa, kb - 1)
        c = block(kb, sb, c, False)
        put_scores(sb, jnp.maximum(kb - 2, 0))
        return block(kb - 1, sa, c, False)

    _, a1, _, a2 = lax.fori_loop(0, (nb + 1) // 2, pair, carry)

    lv = lam_ref[...]
    lam_full = (jnp.exp(jnp.sum(lv[0:1] * lv[1:2], axis=1, keepdims=True))
                - jnp.exp(jnp.sum(lv[2:3] * lv[3:4], axis=1, keepdims=True)) + lam_init)
    o = (a1[:HEAD_DIM] / a1[HEAD_DIM:HEAD_DIM + 1]
         - lam_full * (a2[:HEAD_DIM] / a2[HEAD_DIM:HEAD_DIM + 1]))
    ms = jnp.mean(o * o, axis=0, keepdims=True)
    on = o * lax.rsqrt(ms + EPS) * g_ref[...] * (1.0 - lam_init)
    o_ref[...] = on.T.astype(o_ref.dtype)


def _da_attention(proj, lam_l, diff_norm_l, slopes, lam_init, tq, tk):
    b, s, width = proj.shape
    nblk = width // HEAD_DIM
    q_off, k_off, v_off = nblk - 3 * DA_HEADS, nblk - 2 * DA_HEADS, nblk - DA_HEADS
    return pl.pallas_call(
        functools.partial(_da_kernel, tq=tq, tk=tk, lam_init=lam_init),
        grid_spec=pltpu.PrefetchScalarGridSpec(
            num_scalar_prefetch=1,
            grid=(b, DA_HEADS, s // tq),
            in_specs=[
                pl.BlockSpec((4, DA_HALF), lambda bi, h, i, sl: (0, 0)),
                pl.BlockSpec((None, tq, HEAD_DIM), lambda bi, h, i, sl: (bi, i, q_off + h)),
                pl.BlockSpec((None, s, HEAD_DIM), lambda bi, h, i, sl: (bi, 0, k_off + h)),
                pl.BlockSpec((None, s, HEAD_DIM), lambda bi, h, i, sl: (bi, 0, v_off + h)),
                pl.BlockSpec((HEAD_DIM, 1), lambda bi, h, i, sl: (h, 0)),
            ],
            out_specs=pl.BlockSpec((None, tq, HEAD_DIM), lambda bi, h, i, sl: (bi, i, h)),
            scratch_shapes=[pltpu.VMEM((s, 2 * HEAD_DIM), BF16),
                            pltpu.VMEM((s // tk, HEAD_DIM + DA_SUM_ROWS, tk), BF16),
                            pltpu.VMEM((2, tk, tq), F32), pltpu.VMEM((2, tk, tq), F32)],
        ),
        out_shape=jax.ShapeDtypeStruct((b, s, DA_HEADS * HEAD_DIM), BF16),
        compiler_params=_params(("parallel", "parallel", "arbitrary")),
        name="da_attention",
    )(slopes, lam_l, proj, proj, proj, diff_norm_l.reshape(-1, 1))


def _conv_kernel(a0, a1, g0, g1, ha0, ha1, hg0, hg1, cw_ref, cb_ref, lng_ref, lnb_ref, wpw_ref,
                 o_ref, scr0, scr1, *, ts, half):
    first = pl.program_id(1) == 0
    conv = []
    for c, (a, g, ha, hg, scr) in enumerate(((a0, g0, ha0, hg0, scr0), (a1, g1, ha1, hg1, scr1))):
        lo = c * half
        u = a[...].astype(F32) * jax.nn.sigmoid(g[...].astype(F32))
        hu = ha[...].astype(F32) * jax.nn.sigmoid(hg[...].astype(F32))
        scr[0:CONV_HALO, :] = jnp.where(first, 0.0, hu)
        scr[CONV_HALO:CONV_HALO + ts, :] = u
        acc = jnp.broadcast_to(cb_ref[:, lo:lo + half], (ts, half))
        for j in range(CONV_KERNEL):
            off = CONV_HALO - (CONV_KERNEL - 1) + j
            acc = acc + cw_ref[j:j + 1, lo:lo + half] * scr[off:off + ts, :]
        conv.append(acc)
    width = 2 * half
    mu = (jnp.sum(conv[0], axis=1, keepdims=True) + jnp.sum(conv[1], axis=1, keepdims=True)) / width
    d0 = conv[0] - mu
    d1 = conv[1] - mu
    var = (jnp.sum(d0 * d0, axis=1, keepdims=True) + jnp.sum(d1 * d1, axis=1, keepdims=True)) / width
    inv = lax.rsqrt(var + EPS)
    out = None
    for c, d in enumerate((d0, d1)):
        lo = c * half
        un = d * inv * lng_ref[:, lo:lo + half] + lnb_ref[:, lo:lo + half]
        act = (un * jax.nn.sigmoid(un)).astype(BF16)
        part = _dot(act, wpw_ref[lo:lo + half, :])
        out = part if out is None else out + part
    o_ref[...] = out.astype(o_ref.dtype)


def _conformer_conv(proj, conv_w, conv_b, ln_g, ln_b, w_pw, ts):
    b, s, _ = proj.shape
    width = conv_w.shape[1]
    half = width // 2
    a_blk = 3 * SB_HEADS * HEAD_DIM // half
    g_blk = a_blk + 2
    hpt = ts // CONV_HALO

    def cur(cb):
        return pl.BlockSpec((None, ts, half), lambda bi, i: (bi, i, cb))

    def halo(cb):
        return pl.BlockSpec((None, CONV_HALO, half),
                            lambda bi, i: (bi, jnp.maximum(i * hpt - 1, 0), cb))

    def full(shape):
        return pl.BlockSpec(shape, lambda bi, i: (0, 0))

    return pl.pallas_call(
        functools.partial(_conv_kernel, ts=ts, half=half),
        grid=(b, s // ts),
        in_specs=[cur(a_blk), cur(a_blk + 1), cur(g_blk), cur(g_blk + 1),
                  halo(a_blk), halo(a_blk + 1), halo(g_blk), halo(g_blk + 1),
                  full((CONV_KERNEL, width)), full((1, width)), full((1, width)), full((1, width)),
                  full((width, width))],
        out_specs=pl.BlockSpec((None, ts, width), lambda bi, i: (bi, i, 0)),
        out_shape=jax.ShapeDtypeStruct((b, s, width), BF16),
        scratch_shapes=[pltpu.VMEM((CONV_HALO + ts, half), F32)] * 2,
        compiler_params=_params(("parallel", "arbitrary")),
        name="conformer_conv",
    )(proj, proj, proj, proj, proj, proj, proj, proj,
      conv_w, conv_b.reshape(1, -1), ln_g.reshape(1, -1), ln_b.reshape(1, -1), w_pw)


def _out_proj_kernel(sb_ref, cv_ref, da_ref, x_ref, w_ref, g_ref, xo_ref, ho_ref, *, w_sb, w_cv):
    acc = _dot(sb_ref[...], w_ref[0:w_sb, :])
    acc = acc + _dot(cv_ref[...], w_ref[w_sb:w_sb + w_cv, :])
    acc = acc + _dot(da_ref[...], w_ref[w_sb + w_cv:, :])
    xn = x_ref[...] + acc
    xo_ref[...] = xn
    ho_ref[...] = _rms(xn, g_ref[...]).astype(ho_ref.dtype)


def _out_proj(sb_o, cv_o, da_o, x2d, w_out, g, h_dtype, tm):
    n, d = x2d.shape
    w_sb, w_cv, w_da = sb_o.shape[1], cv_o.shape[1], da_o.shape[1]
    row = lambda width: pl.BlockSpec((tm, width), lambda i: (i, 0))
    return pl.pallas_call(
        functools.partial(_out_proj_kernel, w_sb=w_sb, w_cv=w_cv),
        grid=(n // tm,),
        in_specs=[row(w_sb), row(w_cv), row(w_da), row(d),
                  pl.BlockSpec((d, d), lambda i: (0, 0)),
                  pl.BlockSpec((1, d), lambda i: (0, 0))],
        out_specs=[row(d), row(d)],
        out_shape=[jax.ShapeDtypeStruct((n, d), F32), jax.ShapeDtypeStruct((n, d), h_dtype)],
        compiler_params=_params(("parallel",)),
        name="out_proj",
    )(sb_o, cv_o, da_o, x2d, w_out, g)


def _ffn_kernel(te_ref, tv_ref, x_ref, *rest, residual):
    if residual:
        res_ref, wg_ref, wu_ref, wd_ref, o_ref, xb_scr = rest
    else:
        wg_ref, wu_ref, wd_ref, o_ref, xb_scr = rest
    i = pl.program_id(0)
    f = pl.program_id(1)

    @pl.when(f == 0)
    def _():
        xb_scr[...] = x_ref[...].astype(BF16)
        o_ref[...] = res_ref[...] if residual else jnp.zeros_like(o_ref)

    @pl.when(tv_ref[i] > 0)
    def _():
        x = xb_scr[...]
        g = _dot(x, wg_ref[...])
        u = _dot(x, wu_ref[...])
        a = (g * jax.nn.sigmoid(g) * u).astype(BF16)
        o_ref[...] += _dot(a, wd_ref[...])


def _ffn(x_rows, res, wg, wu, wd, tile_expert, tile_valid, tm, tf):
    rows, d = x_rows.shape
    nf = wg.shape[2] // tf

    def fblk(f, i, tv):
        return jnp.where(tv[i] > 0, f, nf - 1)

    in_specs = [pl.BlockSpec((tm, d), lambda i, f, te, tv: (i, 0))]
    args = [x_rows]
    if res is not None:
        in_specs.append(pl.BlockSpec((tm, d), lambda i, f, te, tv: (i, 0)))
        args.append(res)
    in_specs += [
        pl.BlockSpec((None, d, tf), lambda i, f, te, tv: (te[i], 0, fblk(f, i, tv))),
        pl.BlockSpec((None, d, tf), lambda i, f, te, tv: (te[i], 0, fblk(f, i, tv))),
        pl.BlockSpec((None, tf, d), lambda i, f, te, tv: (te[i], fblk(f, i, tv), 0)),
    ]
    return pl.pallas_call(
        functools.partial(_ffn_kernel, residual=res is not None),
        grid_spec=pltpu.PrefetchScalarGridSpec(
            num_scalar_prefetch=2,
            grid=(rows // tm, nf),
            in_specs=in_specs,
            out_specs=pl.BlockSpec((tm, d), lambda i, f, te, tv: (i, 0)),
            scratch_shapes=[pltpu.VMEM((tm, d), BF16)],
        ),
        out_shape=jax.ShapeDtypeStruct((rows, d), F32),
        compiler_params=_params(("parallel", "arbitrary")),
        name="swiglu_ffn",
    )(tile_expert, tile_valid, *args, wg, wu, wd)


def _router_kernel(h_ref, wr_ref, meta_ref, *, n_experts):
    logits = _dot_nt(wr_ref[...], h_ref[...].astype(BF16))
    e = lax.broadcasted_iota(jnp.int32, logits.shape, 0)
    logits = jnp.where(e < n_experts, logits, -jnp.inf)
    m1 = jnp.max(logits, axis=0, keepdims=True)
    i1 = jnp.min(jnp.where(logits == m1, e, n_experts), axis=0, keepdims=True)
    rest = jnp.where(e == i1, -jnp.inf, logits)
    m2 = jnp.max(rest, axis=0, keepdims=True)
    i2 = jnp.min(jnp.where(rest == m2, e, n_experts), axis=0, keepdims=True)
    t = jnp.exp(m2 - m1)
    g1 = 1.0 / (1.0 + t)
    g2 = t / (1.0 + t)
    r = lax.broadcasted_iota(jnp.int32, meta_ref.shape, 0)
    meta = jnp.where(r == 0, i1.astype(F32), 0.0)
    meta = jnp.where(r == 1, i2.astype(F32), meta)
    meta = jnp.where(r == 2, g1, meta)
    meta = jnp.where(r == 3, g2, meta)
    meta_ref[...] = meta


def _router(h, w_router, tm):
    n, d = h.shape
    n_experts = w_router.shape[1]
    wr = jnp.zeros((16, d), BF16).at[:n_experts].set(w_router.T.astype(BF16))
    return pl.pallas_call(
        functools.partial(_router_kernel, n_experts=n_experts),
        grid=(n // tm,),
        in_specs=[pl.BlockSpec((tm, d), lambda i: (i, 0)),
                  pl.BlockSpec((16, d), lambda i: (0, 0))],
        out_specs=pl.BlockSpec((8, tm), lambda i: (0, i)),
        out_shape=jax.ShapeDtypeStruct((8, n), F32),
        compiler_params=_params(("parallel",)),
        name="router",
    )(h, wr)


def _dispatch_kernel(pos0_ref, pos1_ref, h_ref, xs_in, xs_out, sem, *, tm):
    del xs_in
    base = pl.program_id(0) * tm

    def copies(r):
        t = base + r
        src = h_ref.at[pl.ds(r, 1)]
        return (pltpu.make_async_copy(src, xs_out.at[pl.ds(pos0_ref[t], 1)], sem),
                pltpu.make_async_copy(src, xs_out.at[pl.ds(pos1_ref[t], 1)], sem))

    def start(r, c):
        for cp in copies(r):
            cp.start()
        return c

    lax.fori_loop(0, tm, start, 0, unroll=8)
    for _ in range(2):
        pltpu.make_async_copy(h_ref, xs_out.at[pl.ds(0, tm)], sem).wait()


def _dispatch(h, pos0, pos1, n_slots, tm):
    n, d = h.shape
    xs0 = jnp.zeros((n_slots, d), h.dtype)
    return pl.pallas_call(
        functools.partial(_dispatch_kernel, tm=tm),
        grid_spec=pltpu.PrefetchScalarGridSpec(
            num_scalar_prefetch=2,
            grid=(n // tm,),
            in_specs=[pl.BlockSpec((tm, d), lambda i, p0, p1: (i, 0)),
                      pl.BlockSpec(memory_space=pl.ANY)],
            out_specs=pl.BlockSpec(memory_space=pl.ANY),
            scratch_shapes=[pltpu.SemaphoreType.DMA(())],
        ),
        out_shape=jax.ShapeDtypeStruct((n_slots, d), h.dtype),
        input_output_aliases={3: 0},
        compiler_params=_params(("arbitrary",)),
        name="dispatch_rows",
    )(pos0, pos1, h, xs0)


def _combine_kernel(pos0_ref, pos1_ref, ys_hbm, x_ref, gate_ref, g_ref, o_ref, buf, sem, *, tm, final):
    base = pl.program_id(0) * tm

    def copies(r):
        t = base + r
        return (pltpu.make_async_copy(ys_hbm.at[pl.ds(pos0_ref[t], 1)], buf.at[0, pl.ds(r, 1)], sem),
                pltpu.make_async_copy(ys_hbm.at[pl.ds(pos1_ref[t], 1)], buf.at[1, pl.ds(r, 1)], sem))

    def start(r, c):
        for cp in copies(r):
            cp.start()
        return c

    lax.fori_loop(0, tm, start, 0, unroll=8)
    for k in range(2):
        pltpu.make_async_copy(ys_hbm.at[pl.ds(0, tm)], buf.at[k], sem).wait()
    gates = gate_ref[...]
    y = x_ref[...] + gates[:, 0:1] * buf[0] + gates[:, 1:2] * buf[1]
    o_ref[...] = _rms(y, g_ref[...]) if final else y


def _combine(ys, x2d, gates, pos0, pos1, g, final, tm):
    n, d = x2d.shape
    return pl.pallas_call(
        functools.partial(_combine_kernel, tm=tm, final=final),
        grid_spec=pltpu.PrefetchScalarGridSpec(
            num_scalar_prefetch=2,
            grid=(n // tm,),
            in_specs=[pl.BlockSpec(memory_space=pl.ANY),
                      pl.BlockSpec((tm, d), lambda i, p0, p1: (i, 0)),
                      pl.BlockSpec((tm, 2), lambda i, p0, p1: (i, 0)),
                      pl.BlockSpec((1, d), lambda i, p0, p1: (0, 0))],
            out_specs=pl.BlockSpec((tm, d), lambda i, p0, p1: (i, 0)),
            scratch_shapes=[pltpu.VMEM((2, tm, d), F32), pltpu.SemaphoreType.DMA(())],
        ),
        out_shape=jax.ShapeDtypeStruct((n, d), F32),
        compiler_params=_params(("arbitrary",)),
        name="combine_rows",
    )(pos0, pos1, ys, x2d, gates, g)


def _final_norm_kernel(x_ref, g_ref, o_ref):
    o_ref[...] = _rms(x_ref[...], g_ref[...])


def _final_norm(x2d, g, tm):
    n, d = x2d.shape
    return pl.pallas_call(
        _final_norm_kernel,
        grid=(n // tm,),
        in_specs=[pl.BlockSpec((tm, d), lambda i: (i, 0)), pl.BlockSpec((1, d), lambda i: (0, 0))],
        out_specs=pl.BlockSpec((tm, d), lambda i: (i, 0)),
        out_shape=jax.ShapeDtypeStruct((n, d), F32),
        compiler_params=_params(("parallel",)),
        name="final_norm",
    )(x2d, g)


def _routing_tables(meta, tm):
    n = meta.shape[1]
    experts = meta[0:2].astype(jnp.int32)
    gates = meta[2:4].T
    flat = experts.reshape(-1)
    onehot = (flat[:, None] == jnp.arange(N_EXPERTS)[None, :]).astype(jnp.int32)
    ranks = jnp.cumsum(onehot, axis=0) - onehot
    rank = jnp.sum(ranks * onehot, axis=1)
    counts = jnp.sum(onehot, axis=0)
    tiles = (counts + tm - 1) // tm
    tile_end = jnp.cumsum(tiles)
    tile_start = tile_end - tiles
    pos = (tile_start * tm)[flat] + rank
    n_tiles = (2 * n) // tm + N_EXPERTS
    tile_ids = jnp.arange(n_tiles)
    tile_valid = (tile_ids < tile_end[-1]).astype(jnp.int32)
    clipped = jnp.minimum(tile_ids, tile_end[-1] - 1)
    tile_expert = jnp.sum((clipped[:, None] >= tile_end[None, :]).astype(jnp.int32), axis=1)
    tile_expert = jnp.minimum(tile_expert, N_EXPERTS - 1).astype(jnp.int32)
    return pos[:n].astype(jnp.int32), pos[n:].astype(jnp.int32), gates, tile_expert, tile_valid


TM_IN, TN_IN = 1024, 1408
TQ_SB = 256
TQ_DA, TK_DA = 512, 512
TS_CONV = 512
TM_OUT = 512
TM_FFN, TF_DENSE, TF_EXPERT = 512, 512, 256
TM_ROUTER = 512
TM_ROWS = 512


def kernel(x, attn_norm, w_in, w_out, lam, diff_norm, conv_w, conv_b, conv_ln_g, conv_ln_b, w_conv_out, ffn_norm, w_gate, w_up, w_down, w_router, e_gate, e_up, e_down, final_norm):
    b, s, d = x.shape
    n = b * s
    depth = w_in.shape[0]
    sb_w = SB_HEADS * HEAD_DIM
    da_w = DA_HEADS * HEAD_DIM
    cv_w = conv_w.shape[2]
    in_w = w_in.shape[2]

    colscale = jnp.ones((in_w,), F32)
    colscale = colscale.at[:sb_w].set(HEAD_DIM ** -0.5)
    da_q0 = in_w - 3 * da_w
    colscale = colscale.at[da_q0:da_q0 + da_w].set(DA_HALF ** -0.5 * LOG2E).reshape(1, in_w)
    slopes = jnp.exp2(-8.0 * (jnp.arange(DA_HEADS, dtype=F32) + 1.0) / DA_HEADS) * LOG2E

    x2d = x.reshape(n, d)
    out = None
    for l in range(depth):
        lam_init = 0.8 - 0.6 * math.exp(-0.3 * l)
        moe = l % 2 == 1
        i = l // 2
        proj = _norm_in_proj(x2d, attn_norm[l].reshape(1, d), w_in[l].astype(BF16), colscale,
                             TM_IN, TN_IN).reshape(b, s, in_w)
        sb_o = _sb_attention(proj, TQ_SB).reshape(n, sb_w)
        cv_o = _conformer_conv(proj, conv_w[l], conv_b[l], conv_ln_g[l], conv_ln_b[l],
                               w_conv_out[l].astype(BF16), TS_CONV).reshape(n, cv_w)
        da_o = _da_attention(proj, lam[l], diff_norm[l], slopes, lam_init, TQ_DA, TK_DA).reshape(n, da_w)
        x2d, h = _out_proj(sb_o, cv_o, da_o, x2d, w_out[l].astype(BF16), ffn_norm[l].reshape(1, d),
                           F32 if moe else BF16, TM_OUT)
        if not moe:
            ones = jnp.ones((n // TM_FFN,), jnp.int32)
            x2d = _ffn(h, x2d, w_gate[i].astype(BF16)[None], w_up[i].astype(BF16)[None],
                       w_down[i].astype(BF16)[None], jnp.zeros_like(ones), ones, TM_FFN, TF_DENSE)
            if l == depth - 1:
                out = _final_norm(x2d, final_norm.reshape(1, d), TM_ROWS)
        else:
            meta = _router(h, w_router[i], TM_ROUTER)
            pos0, pos1, gates, tile_expert, tile_valid = _routing_tables(meta, TM_FFN)
            n_slots = 2 * n + N_EXPERTS * TM_FFN
            xs = _dispatch(h, pos0, pos1, n_slots, TM_ROWS)
            ys = _ffn(xs, None, e_gate[i].astype(BF16), e_up[i].astype(BF16), e_down[i].astype(BF16),
                      tile_expert, tile_valid, TM_FFN, TF_EXPERT)
            last = l == depth - 1
            x2d = _combine(ys, x2d, gates, pos0, pos1, final_norm.reshape(1, d), last, TM_ROWS)
            if last:
                out = x2d
    return out.reshape(b, s, d)
```

```python
import functools
import math

import jax
import jax.numpy as jnp
from jax import lax
from jax.experimental import pallas as pl
from jax.experimental.pallas import tpu as pltpu

F32 = jnp.float32
BF16 = jnp.bfloat16

HEAD_DIM = 128
SB_HEADS = 6
DA_HEADS = 6
DA_HALF = HEAD_DIM // 2
CHUNK = 64
CONV_KERNEL = 31
CONV_HALO = 32
SUBLANES = 8
CONV_ROWS = 64
N_EXPERTS = 8
EPS = 1e-6
NEG_BIG = -1e30
SB_EXIT = 104.0
LOG2E = 1.4426950408889634
DA_SUM_ROWS = 16

VMEM_LIMIT = 56 * 1024 * 1024


def _params(sem, vmem=VMEM_LIMIT):
    return pltpu.CompilerParams(dimension_semantics=sem, vmem_limit_bytes=vmem)


def _rms(x, g):
    ms = jnp.mean(x * x, axis=-1, keepdims=True)
    return x * lax.rsqrt(ms + EPS) * g


def _dot(a, b):
    return jnp.dot(a, b, preferred_element_type=F32)


def _dot_nt(a, b):
    return lax.dot_general(a, b, (((1,), (1,)), ((), ())), preferred_element_type=F32)


def _norm_matmul_kernel(x_ref, g_ref, w_ref, cs_ref, o_ref, h_scr):
    @pl.when(pl.program_id(1) == 0)
    def _():
        h_scr[...] = _rms(x_ref[...], g_ref[...]).astype(BF16)

    o_ref[...] = (_dot(h_scr[...], w_ref[...]) * cs_ref[...]).astype(o_ref.dtype)


def _norm_in_proj(x2d, g, w, colscale, tm, tn):
    n, d = x2d.shape
    nout = w.shape[1]
    return pl.pallas_call(
        _norm_matmul_kernel,
        grid=(n // tm, nout // tn),
        in_specs=[
            pl.BlockSpec((tm, d), lambda i, j: (i, 0)),
            pl.BlockSpec((1, d), lambda i, j: (0, 0)),
            pl.BlockSpec((d, tn), lambda i, j: (0, j)),
            pl.BlockSpec((1, tn), lambda i, j: (0, j)),
        ],
        out_specs=pl.BlockSpec((tm, tn), lambda i, j: (i, j)),
        out_shape=jax.ShapeDtypeStruct((n, nout), BF16),
        scratch_shapes=[pltpu.VMEM((tm, d), BF16)],
        compiler_params=_params(("parallel", "arbitrary")),
        name="norm_in_proj",
    )(x2d, g, w, colscale)


def _sb_kernel(q_ref, k_ref, v_ref, o_ref, *, tq, heads):
    qi = pl.program_id(2)
    row = lax.broadcasted_iota(jnp.int32, (tq, tq), 0)
    col = lax.broadcasted_iota(jnp.int32, (tq, tq), 1)
    tri = jnp.where(row > col, 1.0, 0.0).astype(BF16)
    causal = col < row
    lanes = [slice(h * HEAD_DIM, (h + 1) * HEAD_DIM) for h in range(heads)]
    q = [q_ref[:, ln] for ln in lanes]

    def log1m_sigmoid(z):
        return jnp.minimum(-z, 0.0) - jnp.log(1.0 + jnp.exp(-jnp.abs(z)))

    def later_sum(l1m):
        hi = l1m.astype(BF16)
        lo = (l1m - hi.astype(F32)).astype(BF16)
        return _dot(hi, tri) + _dot(lo, tri)

    def block(h, kb, rem, acc, diag):
        start = pl.multiple_of(kb * tq, tq)
        k = k_ref[pl.ds(start, tq), lanes[h]]
        v = v_ref[pl.ds(start, tq), lanes[h]]
        z = _dot_nt(q[h], k)
        l1m = log1m_sigmoid(z)
        if diag:
            l1m = jnp.where(causal, l1m, 0.0)
        after = rem + later_sum(l1m)
        w = jnp.exp(l1m + z + after)
        if diag:
            w = jnp.where(causal, w, 0.0)
        acc = acc + _dot(w.astype(BF16), v)
        rem = rem + jnp.sum(l1m, axis=1, keepdims=True)
        return rem, acc

    rem0 = jnp.zeros((tq, 1), F32)
    acc0 = jnp.zeros((tq, HEAD_DIM), F32)

    @pl.when(qi == 0)
    def _():
        for h in range(heads):
            _, acc = block(h, 0, rem0, acc0, True)
            o_ref[:, lanes[h]] = acc.astype(o_ref.dtype)

    def own_and_previous_block(h):
        start = pl.multiple_of((qi - 1) * tq, tq)
        z = _dot_nt(q[h], k_ref[pl.ds(start, 2 * tq), lanes[h]])
        l1m = log1m_sigmoid(z)
        l_prev = l1m[:, :tq]
        l_own = jnp.where(causal, l1m[:, tq:], 0.0)
        sum_own = jnp.sum(l_own, axis=1, keepdims=True)
        w_own = jnp.where(causal, jnp.exp(l_own + z[:, tq:] + later_sum(l_own)), 0.0)
        w_prev = jnp.exp(l_prev + z[:, :tq] + (later_sum(l_prev) + sum_own))
        w = jnp.concatenate([w_prev, w_own], axis=1).astype(BF16)
        acc = _dot(w, v_ref[pl.ds(start, 2 * tq), lanes[h]])
        return sum_own + jnp.sum(l_prev, axis=1, keepdims=True), acc

    @pl.when(qi > 0)
    def _():
        newest = [own_and_previous_block(h) for h in range(heads)]
        for h, (rem, acc) in enumerate(newest):
            def cond(c):
                return jnp.logical_and(c[0] >= 0, c[1] > -SB_EXIT)

            def body(c, h=h):
                r, a = block(h, c[0], c[2], c[3], False)
                return c[0] - 1, jnp.max(r), r, a

            _, _, _, acc = lax.while_loop(cond, body, (qi - 2, jnp.max(rem), rem, acc))
            o_ref[:, lanes[h]] = acc.astype(o_ref.dtype)


def _sb_attention(proj, tq, heads):
    b, s, _ = proj.shape
    groups = SB_HEADS // heads
    width = heads * HEAD_DIM
    q_off, k_off, v_off = 0, groups, 2 * groups
    return pl.pallas_call(
        functools.partial(_sb_kernel, tq=tq, heads=heads),
        grid=(b, groups, s // tq),
        in_specs=[
            pl.BlockSpec((None, tq, width), lambda bi, g, i: (bi, i, q_off + g)),
            pl.BlockSpec((None, s, width), lambda bi, g, i: (bi, 0, k_off + g)),
            pl.BlockSpec((None, s, width), lambda bi, g, i: (bi, 0, v_off + g)),
        ],
        out_specs=pl.BlockSpec((None, tq, width), lambda bi, g, i: (bi, i, g)),
        out_shape=jax.ShapeDtypeStruct((b, s, SB_HEADS * HEAD_DIM), BF16),
        compiler_params=_params(("parallel", "parallel", "arbitrary")),
        name="sb_attention",
    )(proj, proj, proj)


def _da_position_terms(pos, ns, key_side):
    axis = 1 if key_side else 0
    idx = lax.broadcasted_iota(jnp.int32, pos.shape, axis)
    nh = ns.astype(BF16).astype(F32)
    nl = (ns - nh).astype(BF16).astype(F32)
    hi = (pos // CHUNK).astype(F32)
    lo = (pos % CHUNK).astype(F32)
    sign = 1.0 if key_side else -1.0
    coef = jnp.where(idx % 2 == 0, nh, nl) * jnp.where(idx % 4 < 2, sign * CHUNK, sign)
    posv = jnp.where(idx % 4 < 2, hi, lo)
    first = (idx < 4) if key_side else (idx >= 4)
    return jnp.where(idx < 8, jnp.where(first, coef, posv), 0.0).astype(BF16)


def _da_kernel(slopes_ref, lam_ref, q_ref, k_ref, v_ref, g_ref, o_ref, kaug, vaug, sa, sb, *, tq, tk,
               lam_init):
    h = pl.program_id(1)
    qi = pl.program_id(2)
    nslope = -slopes_ref[h]
    nkb, rows, _ = vaug.shape

    @pl.when(qi == 0)
    def _():
        sub = lax.broadcasted_iota(jnp.int32, (rows - HEAD_DIM, tk), 0)
        tail = jnp.where(sub == 0, 1.0, 0.0).astype(BF16)
        ns = jnp.full((tk, HEAD_DIM), nslope, F32)
        for c in range(nkb):
            vaug[c, 0:HEAD_DIM, :] = v_ref[c * tk:(c + 1) * tk, :].astype(F32).T.astype(BF16)
            vaug[c, HEAD_DIM:rows, :] = tail
            kaug[c * tk:(c + 1) * tk, 0:HEAD_DIM] = k_ref[c * tk:(c + 1) * tk, :]
            pos = c * tk + lax.broadcasted_iota(jnp.int32, (tk, HEAD_DIM), 0)
            kaug[c * tk:(c + 1) * tk, HEAD_DIM:] = _da_position_terms(pos, ns, True)

    qt = q_ref[...].astype(F32).T
    sub = lax.broadcasted_iota(jnp.int32, (HEAD_DIM, tq), 0)
    tpos = qi * tq + lax.broadcasted_iota(jnp.int32, (HEAD_DIM, tq), 1)
    qpos = _da_position_terms(tpos, jnp.full((HEAD_DIM, tq), nslope, F32), False)
    q1 = jnp.concatenate([jnp.where(sub < DA_HALF, qt, 0.0).astype(BF16), qpos], axis=0)
    q2 = jnp.concatenate([jnp.where(sub >= DA_HALF, qt, 0.0).astype(BF16), qpos], axis=0)
    kd = (qi * tq) // tk
    rel = qi * tq - kd * tk

    def update(st, m, acc, vt):
        m_new = jnp.maximum(m, jnp.max(st, axis=0, keepdims=True))
        alpha = jnp.exp2(m - m_new)
        p = jnp.exp2(st - m_new).astype(BF16)
        acc = alpha * acc + _dot(vt, p)
        return m_new, acc

    def put_scores(s_ref, kb):
        k = kaug[pl.ds(pl.multiple_of(kb * tk, tk), tk), :]
        s_ref[0] = _dot(k, q1)
        s_ref[1] = _dot(k, q2)

    def block(kb, s_ref, carry, diag):
        m1, a1, m2, a2 = carry
        vt = vaug[kb]
        s1 = s_ref[0]
        s2 = s_ref[1]
        if diag:
            jj = lax.broadcasted_iota(jnp.int32, (tk, tq), 0)
            ii = lax.broadcasted_iota(jnp.int32, (tk, tq), 1)
            allowed = (jj // CHUNK) <= ((ii + rel) // CHUNK)
            d = (ii + rel - jj).astype(F32)
            fix = jnp.where(d < 0.0, -2.0 * nslope * d, 0.0)
            s1 = jnp.where(allowed, s1 + fix, NEG_BIG)
            s2 = jnp.where(allowed, s2 + fix, NEG_BIG)
        m1, a1 = update(s1, m1, a1, vt)
        m2, a2 = update(s2, m2, a2, vt)
        return m1, a1, m2, a2

    m0 = jnp.full((1, tq), NEG_BIG, F32)
    a0 = jnp.zeros((rows, tq), F32)
    put_scores(sa, kd)
    put_scores(sb, jnp.maximum(kd - 1, 0))
    carry = block(kd, sa, (m0, a0, m0, a0), True)

    def odd_block(c):
        c = block(kd - 1, sb, c, False)
        put_scores(sb, jnp.maximum(kd - 2, 0))
        return c

    carry = lax.cond(kd % 2 == 1, odd_block, lambda c: c, carry)
    nb = kd - 1 - kd % 2

    def pair(j, c):
        kb = nb - 2 * j
        put_scores(sa, kb - 1)
        c = block(kb, sb, c, False)
        put_scores(sb, jnp.maximum(kb - 2, 0))
        return block(kb - 1, sa, c, False)

    _, a1, _, a2 = lax.fori_loop(0, (nb + 1) // 2, pair, carry)

    lv = lam_ref[...]
    lam_full = (jnp.exp(jnp.sum(lv[0:1] * lv[1:2], axis=1, keepdims=True))
                - jnp.exp(jnp.sum(lv[2:3] * lv[3:4], axis=1, keepdims=True)) + lam_init)
    o = (a1[:HEAD_DIM] / a1[HEAD_DIM:HEAD_DIM + 1]
         - lam_full * (a2[:HEAD_DIM] / a2[HEAD_DIM:HEAD_DIM + 1]))
    ms = jnp.mean(o * o, axis=0, keepdims=True)
    on = o * lax.rsqrt(ms + EPS) * g_ref[...] * (1.0 - lam_init)
    o_ref[...] = on.T.astype(o_ref.dtype)


def _da_attention(proj, lam_l, diff_norm_l, slopes, lam_init, tq, tk):
    b, s, width = proj.shape
    nblk = width // HEAD_DIM
    q_off, k_off, v_off = nblk - 3 * DA_HEADS, nblk - 2 * DA_HEADS, nblk - DA_HEADS
    return pl.pallas_call(
        functools.partial(_da_kernel, tq=tq, tk=tk, lam_init=lam_init),
        grid_spec=pltpu.PrefetchScalarGridSpec(
            num_scalar_prefetch=1,
            grid=(b, DA_HEADS, s // tq),
            in_specs=[
                pl.BlockSpec((4, DA_HALF), lambda bi, h, i, sl: (0, 0)),
                pl.BlockSpec((None, tq, HEAD_DIM), lambda bi, h, i, sl: (bi, i, q_off + h)),
                pl.BlockSpec((None, s, HEAD_DIM), lambda bi, h, i, sl: (bi, 0, k_off + h)),
                pl.BlockSpec((None, s, HEAD_DIM), lambda bi, h, i, sl: (bi, 0, v_off + h)),
                pl.BlockSpec((HEAD_DIM, 1), lambda bi, h, i, sl: (h, 0)),
            ],
            out_specs=pl.BlockSpec((None, tq, HEAD_DIM), lambda bi, h, i, sl: (bi, i, h)),
            scratch_shapes=[pltpu.VMEM((s, 2 * HEAD_DIM), BF16),
                            pltpu.VMEM((s // tk, HEAD_DIM + DA_SUM_ROWS, tk), BF16),
                            pltpu.VMEM((2, tk, tq), F32), pltpu.VMEM((2, tk, tq), F32)],
        ),
        out_shape=jax.ShapeDtypeStruct((b, s, DA_HEADS * HEAD_DIM), BF16),
        compiler_params=_params(("parallel", "parallel", "arbitrary")),
        name="da_attention",
    )(slopes, lam_l, proj, proj, proj, diff_norm_l.reshape(-1, 1))


def _conv_kernel(a0, a1, g0, g1, ha0, ha1, hg0, hg1, cw_ref, cb_ref, lng_ref, lnb_ref, wpw_ref,
                 o_ref, scr, shf, act, *, ts, half):
    first = pl.program_id(1) == 0
    offs = [CONV_HALO - (CONV_KERNEL - 1) + j for j in range(CONV_KERNEL)]
    for c, (a, g, ha, hg) in enumerate(((a0, g0, ha0, hg0), (a1, g1, ha1, hg1))):
        u = a[...].astype(F32) * jax.nn.sigmoid(g[...].astype(F32))
        hu = ha[...].astype(F32) * jax.nn.sigmoid(hg[...].astype(F32))
        scr[c, 0:CONV_HALO, :] = jnp.where(first, 0.0, hu)
        scr[c, CONV_HALO:CONV_HALO + ts, :] = u
        for r in range(1, SUBLANES):
            last = max((o for o in offs if o % SUBLANES == r), default=None)
            if last is not None:
                span = last - r + ts
                shf[c, r - 1, 0:span, :] = scr[c, r:r + span, :]

    width = 2 * half
    for c0 in range(0, ts, CONV_ROWS):
        conv = []
        for c in range(2):
            lo = c * half
            acc = jnp.broadcast_to(cb_ref[:, lo:lo + half], (CONV_ROWS, half))
            for j, off in enumerate(offs):
                r = off % SUBLANES
                m = off - r + c0
                rows = scr[c, m:m + CONV_ROWS, :] if r == 0 else shf[c, r - 1, m:m + CONV_ROWS, :]
                acc = acc + cw_ref[j:j + 1, lo:lo + half] * rows
            conv.append(acc)
        mu = (jnp.sum(conv[0], axis=1, keepdims=True) + jnp.sum(conv[1], axis=1, keepdims=True)) / width
        d0 = conv[0] - mu
        d1 = conv[1] - mu
        var = (jnp.sum(d0 * d0, axis=1, keepdims=True) + jnp.sum(d1 * d1, axis=1, keepdims=True)) / width
        inv = lax.rsqrt(var + EPS)
        for c, d in enumerate((d0, d1)):
            lo = c * half
            un = d * inv * lng_ref[:, lo:lo + half] + lnb_ref[:, lo:lo + half]
            act[c0:c0 + CONV_ROWS, lo:lo + half] = (un * jax.nn.sigmoid(un)).astype(BF16)
    o_ref[...] = _dot(act[...], wpw_ref[...]).astype(o_ref.dtype)


def _conformer_conv(proj, conv_w, conv_b, ln_g, ln_b, w_pw, ts):
    b, s, _ = proj.shape
    width = conv_w.shape[1]
    half = width // 2
    a_blk = 3 * SB_HEADS * HEAD_DIM // half
    g_blk = a_blk + 2
    hpt = ts // CONV_HALO

    def cur(cb):
        return pl.BlockSpec((None, ts, half), lambda bi, i: (bi, i, cb))

    def halo(cb):
        return pl.BlockSpec((None, CONV_HALO, half),
                            lambda bi, i: (bi, jnp.maximum(i * hpt - 1, 0), cb))

    def full(shape):
        return pl.BlockSpec(shape, lambda bi, i: (0, 0))

    return pl.pallas_call(
        functools.partial(_conv_kernel, ts=ts, half=half),
        grid=(b, s // ts),
        in_specs=[cur(a_blk), cur(a_blk + 1), cur(g_blk), cur(g_blk + 1),
                  halo(a_blk), halo(a_blk + 1), halo(g_blk), halo(g_blk + 1),
                  full((CONV_KERNEL, width)), full((1, width)), full((1, width)), full((1, width)),
                  full((width, width))],
        out_specs=pl.BlockSpec((None, ts, width), lambda bi, i: (bi, i, 0)),
        out_shape=jax.ShapeDtypeStruct((b, s, width), BF16),
        scratch_shapes=[pltpu.VMEM((2, CONV_HALO + ts, half), F32),
                        pltpu.VMEM((2, SUBLANES - 1, CONV_HALO + ts, half), F32),
                        pltpu.VMEM((ts, width), BF16)],
        compiler_params=_params(("parallel", "arbitrary")),
        name="conformer_conv",
    )(proj, proj, proj, proj, proj, proj, proj, proj,
      conv_w, conv_b.reshape(1, -1), ln_g.reshape(1, -1), ln_b.reshape(1, -1), w_pw)


def _out_proj_kernel(sb_ref, cv_ref, da_ref, x_ref, w_ref, g_ref, xo_ref, ho_ref, *, w_sb, w_cv):
    acc = _dot(sb_ref[...], w_ref[0:w_sb, :])
    acc = acc + _dot(cv_ref[...], w_ref[w_sb:w_sb + w_cv, :])
    acc = acc + _dot(da_ref[...], w_ref[w_sb + w_cv:, :])
    xn = x_ref[...] + acc
    xo_ref[...] = xn
    ho_ref[...] = _rms(xn, g_ref[...]).astype(ho_ref.dtype)


def _out_proj(sb_o, cv_o, da_o, x2d, w_out, g, h_dtype, tm):
    n, d = x2d.shape
    w_sb, w_cv, w_da = sb_o.shape[1], cv_o.shape[1], da_o.shape[1]
    row = lambda width: pl.BlockSpec((tm, width), lambda i: (i, 0))
    return pl.pallas_call(
        functools.partial(_out_proj_kernel, w_sb=w_sb, w_cv=w_cv),
        grid=(n // tm,),
        in_specs=[row(w_sb), row(w_cv), row(w_da), row(d),
                  pl.BlockSpec((d, d), lambda i: (0, 0)),
                  pl.BlockSpec((1, d), lambda i: (0, 0))],
        out_specs=[row(d), row(d)],
        out_shape=[jax.ShapeDtypeStruct((n, d), F32), jax.ShapeDtypeStruct((n, d), h_dtype)],
        compiler_params=_params(("parallel",)),
        name="out_proj",
    )(sb_o, cv_o, da_o, x2d, w_out, g)


def _ffn_kernel(te_ref, tv_ref, x_ref, *rest, residual):
    if residual:
        res_ref, wg_ref, wu_ref, wd_ref, o_ref, xb_scr = rest
    else:
        wg_ref, wu_ref, wd_ref, o_ref, xb_scr = rest
    i = pl.program_id(0)
    f = pl.program_id(1)

    @pl.when(f == 0)
    def _():
        xb_scr[...] = x_ref[...].astype(BF16)
        o_ref[...] = res_ref[...] if residual else jnp.zeros_like(o_ref)

    @pl.when(tv_ref[i] > 0)
    def _():
        x = xb_scr[...]
        g = _dot(x, wg_ref[...])
        u = _dot(x, wu_ref[...])
        a = (g * jax.nn.sigmoid(g) * u).astype(BF16)
        o_ref[...] += _dot(a, wd_ref[...])


def _ffn(x_rows, res, wg, wu, wd, tile_expert, tile_valid, tm, tf):
    rows, d = x_rows.shape
    nf = wg.shape[2] // tf

    def fblk(f, i, tv):
        return jnp.where(tv[i] > 0, f, nf - 1)

    in_specs = [pl.BlockSpec((tm, d), lambda i, f, te, tv: (i, 0))]
    args = [x_rows]
    if res is not None:
        in_specs.append(pl.BlockSpec((tm, d), lambda i, f, te, tv: (i, 0)))
        args.append(res)
    in_specs += [
        pl.BlockSpec((None, d, tf), lambda i, f, te, tv: (te[i], 0, fblk(f, i, tv))),
        pl.BlockSpec((None, d, tf), lambda i, f, te, tv: (te[i], 0, fblk(f, i, tv))),
        pl.BlockSpec((None, tf, d), lambda i, f, te, tv: (te[i], fblk(f, i, tv), 0)),
    ]
    return pl.pallas_call(
        functools.partial(_ffn_kernel, residual=res is not None),
        grid_spec=pltpu.PrefetchScalarGridSpec(
            num_scalar_prefetch=2,
            grid=(rows // tm, nf),
            in_specs=in_specs,
            out_specs=pl.BlockSpec((tm, d), lambda i, f, te, tv: (i, 0)),
            scratch_shapes=[pltpu.VMEM((tm, d), BF16)],
        ),
        out_shape=jax.ShapeDtypeStruct((rows, d), F32),
        compiler_params=_params(("parallel", "arbitrary")),
        name="swiglu_ffn",
    )(tile_expert, tile_valid, *args, wg, wu, wd)


def _router_kernel(h_ref, wr_ref, meta_ref, *, n_experts):
    logits = _dot_nt(wr_ref[...], h_ref[...].astype(BF16))
    e = lax.broadcasted_iota(jnp.int32, logits.shape, 0)
    logits = jnp.where(e < n_experts, logits, -jnp.inf)
    m1 = jnp.max(logits, axis=0, keepdims=True)
    i1 = jnp.min(jnp.where(logits == m1, e, n_experts), axis=0, keepdims=True)
    rest = jnp.where(e == i1, -jnp.inf, logits)
    m2 = jnp.max(rest, axis=0, keepdims=True)
    i2 = jnp.min(jnp.where(rest == m2, e, n_experts), axis=0, keepdims=True)
    t = jnp.exp(m2 - m1)
    g1 = 1.0 / (1.0 + t)
    g2 = t / (1.0 + t)
    r = lax.broadcasted_iota(jnp.int32, meta_ref.shape, 0)
    meta = jnp.where(r == 0, i1.astype(F32), 0.0)
    meta = jnp.where(r == 1, i2.astype(F32), meta)
    meta = jnp.where(r == 2, g1, meta)
    meta = jnp.where(r == 3, g2, meta)
    meta_ref[...] = meta


def _router(h, w_router, tm):
    n, d = h.shape
    n_experts = w_router.shape[1]
    wr = jnp.zeros((16, d), BF16).at[:n_experts].set(w_router.T.astype(BF16))
    return pl.pallas_call(
        functools.partial(_router_kernel, n_experts=n_experts),
        grid=(n // tm,),
        in_specs=[pl.BlockSpec((tm, d), lambda i: (i, 0)),
                  pl.BlockSpec((16, d), lambda i: (0, 0))],
        out_specs=pl.BlockSpec((8, tm), lambda i: (0, i)),
        out_shape=jax.ShapeDtypeStruct((8, n), F32),
        compiler_params=_params(("parallel",)),
        name="router",
    )(h, wr)


def _dispatch_kernel(pos0_ref, pos1_ref, h_ref, xs_in, xs_out, sem, *, tm):
    del xs_in
    base = pl.program_id(0) * tm

    def copies(r):
        t = base + r
        src = h_ref.at[pl.ds(r, 1)]
        return (pltpu.make_async_copy(src, xs_out.at[pl.ds(pos0_ref[t], 1)], sem),
                pltpu.make_async_copy(src, xs_out.at[pl.ds(pos1_ref[t], 1)], sem))

    def start(r, c):
        for cp in copies(r):
            cp.start()
        return c

    lax.fori_loop(0, tm, start, 0, unroll=8)
    for _ in range(2):
        pltpu.make_async_copy(h_ref, xs_out.at[pl.ds(0, tm)], sem).wait()


def _dispatch(h, pos0, pos1, n_slots, tm):
    n, d = h.shape
    xs0 = jnp.zeros((n_slots, d), h.dtype)
    return pl.pallas_call(
        functools.partial(_dispatch_kernel, tm=tm),
        grid_spec=pltpu.PrefetchScalarGridSpec(
            num_scalar_prefetch=2,
            grid=(n // tm,),
            in_specs=[pl.BlockSpec((tm, d), lambda i, p0, p1: (i, 0)),
                      pl.BlockSpec(memory_space=pl.ANY)],
            out_specs=pl.BlockSpec(memory_space=pl.ANY),
            scratch_shapes=[pltpu.SemaphoreType.DMA(())],
        ),
        out_shape=jax.ShapeDtypeStruct((n_slots, d), h.dtype),
        input_output_aliases={3: 0},
        compiler_params=_params(("arbitrary",)),
        name="dispatch_rows",
    )(pos0, pos1, h, xs0)


def _combine_kernel(pos0_ref, pos1_ref, ys_hbm, x_ref, gate_ref, g_ref, o_ref, buf, sem, *, tm, final):
    base = pl.program_id(0) * tm

    def copies(r):
        t = base + r
        return (pltpu.make_async_copy(ys_hbm.at[pl.ds(pos0_ref[t], 1)], buf.at[0, pl.ds(r, 1)], sem),
                pltpu.make_async_copy(ys_hbm.at[pl.ds(pos1_ref[t], 1)], buf.at[1, pl.ds(r, 1)], sem))

    def start(r, c):
        for cp in copies(r):
            cp.start()
        return c

    lax.fori_loop(0, tm, start, 0, unroll=8)
    for k in range(2):
        pltpu.make_async_copy(ys_hbm.at[pl.ds(0, tm)], buf.at[k], sem).wait()
    gates = gate_ref[...]
    y = x_ref[...] + gates[:, 0:1] * buf[0] + gates[:, 1:2] * buf[1]
    o_ref[...] = _rms(y, g_ref[...]) if final else y


def _combine(ys, x2d, gates, pos0, pos1, g, final, tm):
    n, d = x2d.shape
    return pl.pallas_call(
        functools.partial(_combine_kernel, tm=tm, final=final),
        grid_spec=pltpu.PrefetchScalarGridSpec(
            num_scalar_prefetch=2,
            grid=(n // tm,),
            in_specs=[pl.BlockSpec(memory_space=pl.ANY),
                      pl.BlockSpec((tm, d), lambda i, p0, p1: (i, 0)),
                      pl.BlockSpec((tm, 2), lambda i, p0, p1: (i, 0)),
                      pl.BlockSpec((1, d), lambda i, p0, p1: (0, 0))],
            out_specs=pl.BlockSpec((tm, d), lambda i, p0, p1: (i, 0)),
            scratch_shapes=[pltpu.VMEM((2, tm, d), F32), pltpu.SemaphoreType.DMA(())],
        ),
        out_shape=jax.ShapeDtypeStruct((n, d), F32),
        compiler_params=_params(("arbitrary",)),
        name="combine_rows",
    )(pos0, pos1, ys, x2d, gates, g)


def _final_norm_kernel(x_ref, g_ref, o_ref):
    o_ref[...] = _rms(x_ref[...], g_ref[...])


def _final_norm(x2d, g, tm):
    n, d = x2d.shape
    return pl.pallas_call(
        _final_norm_kernel,
        grid=(n // tm,),
        in_specs=[pl.BlockSpec((tm, d), lambda i: (i, 0)), pl.BlockSpec((1, d), lambda i: (0, 0))],
        out_specs=pl.BlockSpec((tm, d), lambda i: (i, 0)),
        out_shape=jax.ShapeDtypeStruct((n, d), F32),
        compiler_params=_params(("parallel",)),
        name="final_norm",
    )(x2d, g)


def _routing_tables(meta, tm):
    n = meta.shape[1]
    experts = meta[0:2].astype(jnp.int32)
    gates = meta[2:4].T
    flat = experts.reshape(-1)
    onehot = (flat[:, None] == jnp.arange(N_EXPERTS)[None, :]).astype(jnp.int32)
    ranks = jnp.cumsum(onehot, axis=0) - onehot
    rank = jnp.sum(ranks * onehot, axis=1)
    counts = jnp.sum(onehot, axis=0)
    tiles = (counts + tm - 1) // tm
    tile_end = jnp.cumsum(tiles)
    tile_start = tile_end - tiles
    pos = (tile_start * tm)[flat] + rank
    n_tiles = (2 * n) // tm + N_EXPERTS
    tile_ids = jnp.arange(n_tiles)
    tile_valid = (tile_ids < tile_end[-1]).astype(jnp.int32)
    clipped = jnp.minimum(tile_ids, tile_end[-1] - 1)
    tile_expert = jnp.sum((clipped[:, None] >= tile_end[None, :]).astype(jnp.int32), axis=1)
    tile_expert = jnp.minimum(tile_expert, N_EXPERTS - 1).astype(jnp.int32)
    return pos[:n].astype(jnp.int32), pos[n:].astype(jnp.int32), gates, tile_expert, tile_valid


TM_IN, TN_IN = 512, 2816
TQ_SB, SB_HEADS_PER_STEP = 256, 3
TQ_DA, TK_DA = 512, 512
TS_CONV = 512
TM_OUT = 512
TM_FFN, TF_DENSE, TF_EXPERT = 512, 512, 256
TM_ROUTER = 512
TM_ROWS = 512


def kernel(x, attn_norm, w_in, w_out, lam, diff_norm, conv_w, conv_b, conv_ln_g, conv_ln_b, w_conv_out, ffn_norm, w_gate, w_up, w_down, w_router, e_gate, e_up, e_down, final_norm):
    b, s, d = x.shape
    n = b * s
    depth = w_in.shape[0]
    sb_w = SB_HEADS * HEAD_DIM
    da_w = DA_HEADS * HEAD_DIM
    cv_w = conv_w.shape[2]
    in_w = w_in.shape[2]

    colscale = jnp.ones((in_w,), F32)
    colscale = colscale.at[:sb_w].set(HEAD_DIM ** -0.5)
    da_q0 = in_w - 3 * da_w
    colscale = colscale.at[da_q0:da_q0 + da_w].set(DA_HALF ** -0.5 * LOG2E).reshape(1, in_w)
    slopes = jnp.exp2(-8.0 * (jnp.arange(DA_HEADS, dtype=F32) + 1.0) / DA_HEADS) * LOG2E

    x2d = x.reshape(n, d)
    out = None
    for l in range(depth):
        lam_init = 0.8 - 0.6 * math.exp(-0.3 * l)
        moe = l % 2 == 1
        i = l // 2
        proj = _norm_in_proj(x2d, attn_norm[l].reshape(1, d), w_in[l].astype(BF16), colscale,
                             TM_IN, TN_IN).reshape(b, s, in_w)
        sb_o = _sb_attention(proj, TQ_SB, SB_HEADS_PER_STEP).reshape(n, sb_w)
        cv_o = _conformer_conv(proj, conv_w[l], conv_b[l], conv_ln_g[l], conv_ln_b[l],
                               w_conv_out[l].astype(BF16), TS_CONV).reshape(n, cv_w)
        da_o = _da_attention(proj, lam[l], diff_norm[l], slopes, lam_init, TQ_DA, TK_DA).reshape(n, da_w)
        x2d, h = _out_proj(sb_o, cv_o, da_o, x2d, w_out[l].astype(BF16), ffn_norm[l].reshape(1, d),
                           F32 if moe else BF16, TM_OUT)
        if not moe:
            ones = jnp.ones((n // TM_FFN,), jnp.int32)
            x2d = _ffn(h, x2d, w_gate[i].astype(BF16)[None], w_up[i].astype(BF16)[None],
                       w_down[i].astype(BF16)[None], jnp.zeros_like(ones), ones, TM_FFN, TF_DENSE)
            if l == depth - 1:
                out = _final_norm(x2d, final_norm.reshape(1, d), TM_ROWS)
        else:
            meta = _router(h, w_router[i], TM_ROUTER)
            pos0, pos1, gates, tile_expert, tile_valid = _routing_tables(meta, TM_FFN)
            n_slots = 2 * n + N_EXPERTS * TM_FFN
            xs = _dispatch(h, pos0, pos1, n_slots, TM_ROWS)
            ys = _ffn(xs, None, e_gate[i].astype(BF16), e_up[i].astype(BF16), e_down[i].astype(BF16),
                      tile_expert, tile_valid, TM_FFN, TF_EXPERT)
            last = l == depth - 1
            x2d = _combine(ys, x2d, gates, pos0, pos1, final_norm.reshape(1, d), last, TM_ROWS)
            if last:
                out = x2d
    return out.reshape(b, s, d)
```

```python
import functools
import math

import jax
import jax.numpy as jnp
from jax import lax
from jax.experimental import pallas as pl
from jax.experimental.pallas import tpu as pltpu

F32 = jnp.float32
BF16 = jnp.bfloat16

HEAD_DIM = 128
SB_HEADS = 6
DA_HEADS = 6
DA_HALF = HEAD_DIM // 2
CHUNK = 64
CONV_KERNEL = 31
CONV_HALO = 32
SUBLANES = 8
CONV_ROWS = 64
N_EXPERTS = 8
EPS = 1e-6
NEG_BIG = -1e30
SB_EXIT = 104.0
LOG2E = 1.4426950408889634
DA_SUM_ROWS = 16

VMEM_LIMIT = 56 * 1024 * 1024


def _params(sem, vmem=VMEM_LIMIT):
    return pltpu.CompilerParams(dimension_semantics=sem, vmem_limit_bytes=vmem)


def _rms(x, g):
    ms = jnp.mean(x * x, axis=-1, keepdims=True)
    return x * lax.rsqrt(ms + EPS) * g


def _dot(a, b):
    return jnp.dot(a, b, preferred_element_type=F32)


def _dot_nt(a, b):
    return lax.dot_general(a, b, (((1,), (1,)), ((), ())), preferred_element_type=F32)


def _norm_matmul_kernel(x_ref, g_ref, w_ref, cs_ref, o_ref, h_scr):
    @pl.when(pl.program_id(1) == 0)
    def _():
        h_scr[...] = _rms(x_ref[...], g_ref[...]).astype(BF16)

    o_ref[...] = (_dot(h_scr[...], w_ref[...]) * cs_ref[...]).astype(o_ref.dtype)


def _norm_in_proj(x2d, g, w, colscale, tm, tn):
    n, d = x2d.shape
    nout = w.shape[1]
    return pl.pallas_call(
        _norm_matmul_kernel,
        grid=(n // tm, nout // tn),
        in_specs=[
            pl.BlockSpec((tm, d), lambda i, j: (i, 0)),
            pl.BlockSpec((1, d), lambda i, j: (0, 0)),
            pl.BlockSpec((d, tn), lambda i, j: (0, j)),
            pl.BlockSpec((1, tn), lambda i, j: (0, j)),
        ],
        out_specs=pl.BlockSpec((tm, tn), lambda i, j: (i, j)),
        out_shape=jax.ShapeDtypeStruct((n, nout), BF16),
        scratch_shapes=[pltpu.VMEM((tm, d), BF16)],
        compiler_params=_params(("parallel", "arbitrary")),
        name="norm_in_proj",
    )(x2d, g, w, colscale)


def _sb_kernel(q_ref, k_ref, v_ref, o_ref, *, tq, heads):
    qi = pl.program_id(2)
    row = lax.broadcasted_iota(jnp.int32, (tq, tq), 0)
    col = lax.broadcasted_iota(jnp.int32, (tq, tq), 1)
    tri = jnp.where(row > col, 1.0, 0.0).astype(BF16)
    causal = col < row
    lanes = [slice(h * HEAD_DIM, (h + 1) * HEAD_DIM) for h in range(heads)]
    q = [q_ref[:, ln] for ln in lanes]

    def log1m_sigmoid(z):
        return jnp.minimum(-z, 0.0) - jnp.log(1.0 + jnp.exp(-jnp.abs(z)))

    def later_sum(l1m):
        hi = l1m.astype(BF16)
        lo = (l1m - hi.astype(F32)).astype(BF16)
        return _dot(hi, tri) + _dot(lo, tri)

    def block(h, kb, rem, acc, diag):
        start = pl.multiple_of(kb * tq, tq)
        k = k_ref[pl.ds(start, tq), lanes[h]]
        v = v_ref[pl.ds(start, tq), lanes[h]]
        z = _dot_nt(q[h], k)
        l1m = log1m_sigmoid(z)
        if diag:
            l1m = jnp.where(causal, l1m, 0.0)
        after = rem + later_sum(l1m)
        w = jnp.exp(l1m + z + after)
        if diag:
            w = jnp.where(causal, w, 0.0)
        acc = acc + _dot(w.astype(BF16), v)
        rem = rem + jnp.sum(l1m, axis=1, keepdims=True)
        return rem, acc

    rem0 = jnp.zeros((tq, 1), F32)
    acc0 = jnp.zeros((tq, HEAD_DIM), F32)

    @pl.when(qi == 0)
    def _():
        for h in range(heads):
            _, acc = block(h, 0, rem0, acc0, True)
            o_ref[:, lanes[h]] = acc.astype(o_ref.dtype)

    def own_and_previous_block(h):
        start = pl.multiple_of((qi - 1) * tq, tq)
        z = _dot_nt(q[h], k_ref[pl.ds(start, 2 * tq), lanes[h]])
        l1m = log1m_sigmoid(z)
        l_prev = l1m[:, :tq]
        l_own = jnp.where(causal, l1m[:, tq:], 0.0)
        sum_own = jnp.sum(l_own, axis=1, keepdims=True)
        w_own = jnp.where(causal, jnp.exp(l_own + z[:, tq:] + later_sum(l_own)), 0.0)
        w_prev = jnp.exp(l_prev + z[:, :tq] + (later_sum(l_prev) + sum_own))
        w = jnp.concatenate([w_prev, w_own], axis=1).astype(BF16)
        acc = _dot(w, v_ref[pl.ds(start, 2 * tq), lanes[h]])
        return sum_own + jnp.sum(l_prev, axis=1, keepdims=True), acc

    @pl.when(qi > 0)
    def _():
        newest = [own_and_previous_block(h) for h in range(heads)]
        for h, (rem, acc) in enumerate(newest):
            def cond(c):
                return jnp.logical_and(c[0] >= 0, c[1] > -SB_EXIT)

            def body(c, h=h):
                r, a = block(h, c[0], c[2], c[3], False)
                return c[0] - 1, jnp.max(r), r, a

            _, _, _, acc = lax.while_loop(cond, body, (qi - 2, jnp.max(rem), rem, acc))
            o_ref[:, lanes[h]] = acc.astype(o_ref.dtype)


def _sb_attention(proj, tq, heads):
    b, s, _ = proj.shape
    groups = SB_HEADS // heads
    width = heads * HEAD_DIM
    q_off, k_off, v_off = 0, groups, 2 * groups
    return pl.pallas_call(
        functools.partial(_sb_kernel, tq=tq, heads=heads),
        grid=(b, groups, s // tq),
        in_specs=[
            pl.BlockSpec((None, tq, width), lambda bi, g, i: (bi, i, q_off + g)),
            pl.BlockSpec((None, s, width), lambda bi, g, i: (bi, 0, k_off + g)),
            pl.BlockSpec((None, s, width), lambda bi, g, i: (bi, 0, v_off + g)),
        ],
        out_specs=pl.BlockSpec((None, tq, width), lambda bi, g, i: (bi, i, g)),
        out_shape=jax.ShapeDtypeStruct((b, s, SB_HEADS * HEAD_DIM), BF16),
        compiler_params=_params(("parallel", "parallel", "arbitrary")),
        name="sb_attention",
    )(proj, proj, proj)


def _da_position_terms(pos, ns, key_side):
    axis = 1 if key_side else 0
    idx = lax.broadcasted_iota(jnp.int32, pos.shape, axis)
    nh = ns.astype(BF16).astype(F32)
    nl = (ns - nh).astype(BF16).astype(F32)
    hi = (pos // CHUNK).astype(F32)
    lo = (pos % CHUNK).astype(F32)
    sign = 1.0 if key_side else -1.0
    coef = jnp.where(idx % 2 == 0, nh, nl) * jnp.where(idx % 4 < 2, sign * CHUNK, sign)
    posv = jnp.where(idx % 4 < 2, hi, lo)
    first = (idx < 4) if key_side else (idx >= 4)
    return jnp.where(idx < 8, jnp.where(first, coef, posv), 0.0).astype(BF16)


def _da_kernel(slopes_ref, lam_ref, q_ref, k_ref, v_ref, g_ref, o_ref, kaug, vaug, sa, sb, own_fix, *, tq, tk,
               lam_init):
    h = pl.program_id(1)
    qi = pl.program_id(2)
    nslope = -slopes_ref[h]
    nkb, rows, _ = vaug.shape

    @pl.when(qi == 0)
    def _():
        sub = lax.broadcasted_iota(jnp.int32, (rows - HEAD_DIM, tk), 0)
        tail = jnp.where(sub == 0, 1.0, 0.0).astype(BF16)
        ns = jnp.full((tk, HEAD_DIM), nslope, F32)
        for c in range(nkb):
            vaug[c, 0:HEAD_DIM, :] = v_ref[c * tk:(c + 1) * tk, :].astype(F32).T.astype(BF16)
            vaug[c, HEAD_DIM:rows, :] = tail
            kaug[c * tk:(c + 1) * tk, 0:HEAD_DIM] = k_ref[c * tk:(c + 1) * tk, :]
            pos = c * tk + lax.broadcasted_iota(jnp.int32, (tk, HEAD_DIM), 0)
            kaug[c * tk:(c + 1) * tk, HEAD_DIM:] = _da_position_terms(pos, ns, True)
        jj = lax.broadcasted_iota(jnp.int32, (tk, tq), 0)
        ii = lax.broadcasted_iota(jnp.int32, (tk, tq), 1)
        for v in range(tk // tq):
            allowed = (jj // CHUNK) <= ((ii + v * tq) // CHUNK)
            d = (ii + v * tq - jj).astype(F32)
            own_fix[v] = jnp.where(allowed, jnp.where(d < 0.0, -2.0 * nslope * d, 0.0), NEG_BIG)

    qt = q_ref[...].astype(F32).T
    sub = lax.broadcasted_iota(jnp.int32, (HEAD_DIM, tq), 0)
    tpos = qi * tq + lax.broadcasted_iota(jnp.int32, (HEAD_DIM, tq), 1)
    qpos = _da_position_terms(tpos, jnp.full((HEAD_DIM, tq), nslope, F32), False)
    q1 = jnp.concatenate([jnp.where(sub < DA_HALF, qt, 0.0).astype(BF16), qpos], axis=0)
    q2 = jnp.concatenate([jnp.where(sub >= DA_HALF, qt, 0.0).astype(BF16), qpos], axis=0)
    kd = (qi * tq) // tk
    rel = qi * tq - kd * tk

    def update(st, m, acc, vt):
        m_new = jnp.maximum(m, jnp.max(st, axis=0, keepdims=True))
        alpha = jnp.exp2(m - m_new)
        p = jnp.exp2(st - m_new).astype(BF16)
        acc = alpha * acc + _dot(vt, p)
        return m_new, acc

    def put_scores(s_ref, kb):
        k = kaug[pl.ds(pl.multiple_of(kb * tk, tk), tk), :]
        s_ref[0] = _dot(k, q1)
        s_ref[1] = _dot(k, q2)

    def block(kb, s_ref, carry, diag):
        m1, a1, m2, a2 = carry
        vt = vaug[kb]
        s1 = s_ref[0]
        s2 = s_ref[1]
        if diag:
            fix = own_fix[rel // tq]
            s1 = s1 + fix
            s2 = s2 + fix
        m1, a1 = update(s1, m1, a1, vt)
        m2, a2 = update(s2, m2, a2, vt)
        return m1, a1, m2, a2

    m0 = jnp.full((1, tq), NEG_BIG, F32)
    a0 = jnp.zeros((rows, tq), F32)
    put_scores(sa, kd)
    put_scores(sb, jnp.maximum(kd - 1, 0))
    carry = block(kd, sa, (m0, a0, m0, a0), True)

    def odd_block(c):
        c = block(kd - 1, sb, c, False)
        put_scores(sb, jnp.maximum(kd - 2, 0))
        return c

    carry = lax.cond(kd % 2 == 1, odd_block, lambda c: c, carry)
    nb = kd - 1 - kd % 2

    def pair(j, c):
        kb = nb - 2 * j
        put_scores(sa, kb - 1)
        c = block(kb, sb, c, False)
        put_scores(sb, jnp.maximum(kb - 2, 0))
        return block(kb - 1, sa, c, False)

    _, a1, _, a2 = lax.fori_loop(0, (nb + 1) // 2, pair, carry)

    lv = lam_ref[...]
    lam_full = (jnp.exp(jnp.sum(lv[0:1] * lv[1:2], axis=1, keepdims=True))
                - jnp.exp(jnp.sum(lv[2:3] * lv[3:4], axis=1, keepdims=True)) + lam_init)
    o = (a1[:HEAD_DIM] / a1[HEAD_DIM:HEAD_DIM + 1]
         - lam_full * (a2[:HEAD_DIM] / a2[HEAD_DIM:HEAD_DIM + 1]))
    ms = jnp.mean(o * o, axis=0, keepdims=True)
    on = o * lax.rsqrt(ms + EPS) * g_ref[...] * (1.0 - lam_init)
    o_ref[...] = on.T.astype(o_ref.dtype)


def _da_attention(proj, lam_l, diff_norm_l, slopes, lam_init, tq, tk):
    b, s, width = proj.shape
    nblk = width // HEAD_DIM
    q_off, k_off, v_off = nblk - 3 * DA_HEADS, nblk - 2 * DA_HEADS, nblk - DA_HEADS
    return pl.pallas_call(
        functools.partial(_da_kernel, tq=tq, tk=tk, lam_init=lam_init),
        grid_spec=pltpu.PrefetchScalarGridSpec(
            num_scalar_prefetch=1,
            grid=(b, DA_HEADS, s // tq),
            in_specs=[
                pl.BlockSpec((4, DA_HALF), lambda bi, h, i, sl: (0, 0)),
                pl.BlockSpec((None, tq, HEAD_DIM), lambda bi, h, i, sl: (bi, i, q_off + h)),
                pl.BlockSpec((None, s, HEAD_DIM), lambda bi, h, i, sl: (bi, 0, k_off + h)),
                pl.BlockSpec((None, s, HEAD_DIM), lambda bi, h, i, sl: (bi, 0, v_off + h)),
                pl.BlockSpec((HEAD_DIM, 1), lambda bi, h, i, sl: (h, 0)),
            ],
            out_specs=pl.BlockSpec((None, tq, HEAD_DIM), lambda bi, h, i, sl: (bi, i, h)),
            scratch_shapes=[pltpu.VMEM((s, 2 * HEAD_DIM), BF16),
                            pltpu.VMEM((s // tk, HEAD_DIM + DA_SUM_ROWS, tk), BF16),
                            pltpu.VMEM((2, tk, tq), F32), pltpu.VMEM((2, tk, tq), F32),
                            pltpu.VMEM((tk // tq, tk, tq), F32)],
        ),
        out_shape=jax.ShapeDtypeStruct((b, s, DA_HEADS * HEAD_DIM), BF16),
        compiler_params=_params(("parallel", "parallel", "arbitrary")),
        name="da_attention",
    )(slopes, lam_l, proj, proj, proj, diff_norm_l.reshape(-1, 1))


def _conv_kernel(a0, a1, g0, g1, ha0, ha1, hg0, hg1, cw_ref, cb_ref, lng_ref, lnb_ref, wpw_ref,
                 o_ref, scr, shf, act, *, ts, half):
    first = pl.program_id(1) == 0
    offs = [CONV_HALO - (CONV_KERNEL - 1) + j for j in range(CONV_KERNEL)]
    for c, (a, g, ha, hg) in enumerate(((a0, g0, ha0, hg0), (a1, g1, ha1, hg1))):
        u = a[...].astype(F32) * jax.nn.sigmoid(g[...].astype(F32))
        hu = ha[...].astype(F32) * jax.nn.sigmoid(hg[...].astype(F32))
        scr[c, 0:CONV_HALO, :] = jnp.where(first, 0.0, hu)
        scr[c, CONV_HALO:CONV_HALO + ts, :] = u
        for r in range(1, SUBLANES):
            last = max((o for o in offs if o % SUBLANES == r), default=None)
            if last is not None:
                span = last - r + ts
                shf[c, r - 1, 0:span, :] = scr[c, r:r + span, :]

    width = 2 * half
    for c0 in range(0, ts, CONV_ROWS):
        conv = []
        for c in range(2):
            lo = c * half
            acc = jnp.broadcast_to(cb_ref[:, lo:lo + half], (CONV_ROWS, half))
            for j, off in enumerate(offs):
                r = off % SUBLANES
                m = off - r + c0
                rows = scr[c, m:m + CONV_ROWS, :] if r == 0 else shf[c, r - 1, m:m + CONV_ROWS, :]
                acc = acc + cw_ref[j:j + 1, lo:lo + half] * rows
            conv.append(acc)
        mu = (jnp.sum(conv[0], axis=1, keepdims=True) + jnp.sum(conv[1], axis=1, keepdims=True)) / width
        d0 = conv[0] - mu
        d1 = conv[1] - mu
        var = (jnp.sum(d0 * d0, axis=1, keepdims=True) + jnp.sum(d1 * d1, axis=1, keepdims=True)) / width
        inv = lax.rsqrt(var + EPS)
        for c, d in enumerate((d0, d1)):
            lo = c * half
            un = d * inv * lng_ref[:, lo:lo + half] + lnb_ref[:, lo:lo + half]
            act[c0:c0 + CONV_ROWS, lo:lo + half] = (un * jax.nn.sigmoid(un)).astype(BF16)
    o_ref[...] = _dot(act[...], wpw_ref[...]).astype(o_ref.dtype)


def _conformer_conv(proj, conv_w, conv_b, ln_g, ln_b, w_pw, ts):
    b, s, _ = proj.shape
    width = conv_w.shape[1]
    half = width // 2
    a_blk = 3 * SB_HEADS * HEAD_DIM // half
    g_blk = a_blk + 2
    hpt = ts // CONV_HALO

    def cur(cb):
        return pl.BlockSpec((None, ts, half), lambda bi, i: (bi, i, cb))

    def halo(cb):
        return pl.BlockSpec((None, CONV_HALO, half),
                            lambda bi, i: (bi, jnp.maximum(i * hpt - 1, 0), cb))

    def full(shape):
        return pl.BlockSpec(shape, lambda bi, i: (0, 0))

    return pl.pallas_call(
        functools.partial(_conv_kernel, ts=ts, half=half),
        grid=(b, s // ts),
        in_specs=[cur(a_blk), cur(a_blk + 1), cur(g_blk), cur(g_blk + 1),
                  halo(a_blk), halo(a_blk + 1), halo(g_blk), halo(g_blk + 1),
                  full((CONV_KERNEL, width)), full((1, width)), full((1, width)), full((1, width)),
                  full((width, width))],
        out_specs=pl.BlockSpec((None, ts, width), lambda bi, i: (bi, i, 0)),
        out_shape=jax.ShapeDtypeStruct((b, s, width), BF16),
        scratch_shapes=[pltpu.VMEM((2, CONV_HALO + ts, half), F32),
                        pltpu.VMEM((2, SUBLANES - 1, CONV_HALO + ts, half), F32),
                        pltpu.VMEM((ts, width), BF16)],
        compiler_params=_params(("parallel", "arbitrary")),
        name="conformer_conv",
    )(proj, proj, proj, proj, proj, proj, proj, proj,
      conv_w, conv_b.reshape(1, -1), ln_g.reshape(1, -1), ln_b.reshape(1, -1), w_pw)


def _top2_meta(logits, n_experts, rows):
    e = lax.broadcasted_iota(jnp.int32, logits.shape, 0)
    logits = jnp.where(e < n_experts, logits, -jnp.inf)
    m1 = jnp.max(logits, axis=0, keepdims=True)
    i1 = jnp.min(jnp.where(logits == m1, e, n_experts), axis=0, keepdims=True)
    rest = jnp.where(e == i1, -jnp.inf, logits)
    m2 = jnp.max(rest, axis=0, keepdims=True)
    i2 = jnp.min(jnp.where(rest == m2, e, n_experts), axis=0, keepdims=True)
    t = jnp.exp(m2 - m1)
    g1 = 1.0 / (1.0 + t)
    g2 = t / (1.0 + t)
    r = lax.broadcasted_iota(jnp.int32, (rows, logits.shape[1]), 0)
    meta = jnp.where(r == 0, i1.astype(F32), 0.0)
    meta = jnp.where(r == 1, i2.astype(F32), meta)
    meta = jnp.where(r == 2, g1, meta)
    return jnp.where(r == 3, g2, meta)


def _out_proj_kernel(sb_ref, cv_ref, da_ref, x_ref, w_ref, g_ref, *rest, w_sb, w_cv, n_experts):
    if n_experts:
        wr_ref, xo_ref, ho_ref, meta_ref = rest
    else:
        xo_ref, ho_ref = rest
    acc = _dot(sb_ref[...], w_ref[0:w_sb, :])
    acc = acc + _dot(cv_ref[...], w_ref[w_sb:w_sb + w_cv, :])
    acc = acc + _dot(da_ref[...], w_ref[w_sb + w_cv:, :])
    xn = x_ref[...] + acc
    xo_ref[...] = xn
    h = _rms(xn, g_ref[...])
    ho_ref[...] = h.astype(ho_ref.dtype)
    if n_experts:
        logits = _dot_nt(wr_ref[...], h.astype(BF16))
        meta_ref[...] = _top2_meta(logits, n_experts, meta_ref.shape[0])


def _out_proj(sb_o, cv_o, da_o, x2d, w_out, g, h_dtype, tm, w_router=None):
    n, d = x2d.shape
    w_sb, w_cv, w_da = sb_o.shape[1], cv_o.shape[1], da_o.shape[1]
    row = lambda width: pl.BlockSpec((tm, width), lambda i: (i, 0))
    in_specs = [row(w_sb), row(w_cv), row(w_da), row(d),
                pl.BlockSpec((d, d), lambda i: (0, 0)),
                pl.BlockSpec((1, d), lambda i: (0, 0))]
    out_specs = [row(d), row(d)]
    out_shape = [jax.ShapeDtypeStruct((n, d), F32), jax.ShapeDtypeStruct((n, d), h_dtype)]
    args = [sb_o, cv_o, da_o, x2d, w_out, g]
    n_experts = 0
    if w_router is not None:
        n_experts = w_router.shape[1]
        args.append(jnp.zeros((16, d), BF16).at[:n_experts].set(w_router.T.astype(BF16)))
        in_specs.append(pl.BlockSpec((16, d), lambda i: (0, 0)))
        out_specs.append(pl.BlockSpec((8, tm), lambda i: (0, i)))
        out_shape.append(jax.ShapeDtypeStruct((8, n), F32))
    return pl.pallas_call(
        functools.partial(_out_proj_kernel, w_sb=w_sb, w_cv=w_cv, n_experts=n_experts),
        grid=(n // tm,),
        in_specs=in_specs,
        out_specs=out_specs,
        out_shape=out_shape,
        compiler_params=_params(("parallel",)),
        name="out_proj",
    )(*args)


def _ffn_kernel(te_ref, tv_ref, x_ref, *rest, residual):
    if residual:
        res_ref, wg_ref, wu_ref, wd_ref, o_ref, xb_scr = rest
    else:
        wg_ref, wu_ref, wd_ref, o_ref, xb_scr = rest
    i = pl.program_id(0)
    f = pl.program_id(1)

    @pl.when(f == 0)
    def _():
        xb_scr[...] = x_ref[...].astype(BF16)
        o_ref[...] = res_ref[...] if residual else jnp.zeros_like(o_ref)

    @pl.when(tv_ref[i] > 0)
    def _():
        x = xb_scr[...]
        g = _dot(x, wg_ref[...])
        u = _dot(x, wu_ref[...])
        a = (g * jax.nn.sigmoid(g) * u).astype(BF16)
        o_ref[...] += _dot(a, wd_ref[...])


def _ffn(x_rows, res, wg, wu, wd, tile_expert, tile_valid, tm, tf):
    rows, d = x_rows.shape
    nf = wg.shape[2] // tf

    def fblk(f, i, tv):
        return jnp.where(tv[i] > 0, f, nf - 1)

    in_specs = [pl.BlockSpec((tm, d), lambda i, f, te, tv: (i, 0))]
    args = [x_rows]
    if res is not None:
        in_specs.append(pl.BlockSpec((tm, d), lambda i, f, te, tv: (i, 0)))
        args.append(res)
    in_specs += [
        pl.BlockSpec((None, d, tf), lambda i, f, te, tv: (te[i], 0, fblk(f, i, tv))),
        pl.BlockSpec((None, d, tf), lambda i, f, te, tv: (te[i], 0, fblk(f, i, tv))),
        pl.BlockSpec((None, tf, d), lambda i, f, te, tv: (te[i], fblk(f, i, tv), 0)),
    ]
    return pl.pallas_call(
        functools.partial(_ffn_kernel, residual=res is not None),
        grid_spec=pltpu.PrefetchScalarGridSpec(
            num_scalar_prefetch=2,
            grid=(rows // tm, nf),
            in_specs=in_specs,
            out_specs=pl.BlockSpec((tm, d), lambda i, f, te, tv: (i, 0)),
            scratch_shapes=[pltpu.VMEM((tm, d), BF16)],
        ),
        out_shape=jax.ShapeDtypeStruct((rows, d), F32),
        compiler_params=_params(("parallel", "arbitrary")),
        name="swiglu_ffn",
    )(tile_expert, tile_valid, *args, wg, wu, wd)


def _dispatch_kernel(pos0_ref, pos1_ref, h_ref, xs_in, xs_out, sem, *, tm):
    del xs_in
    base = pl.program_id(0) * tm

    def copies(r):
        t = base + r
        src = h_ref.at[pl.ds(r, 1)]
        return (pltpu.make_async_copy(src, xs_out.at[pl.ds(pos0_ref[t], 1)], sem),
                pltpu.make_async_copy(src, xs_out.at[pl.ds(pos1_ref[t], 1)], sem))

    def start(r, c):
        for cp in copies(r):
            cp.start()
        return c

    lax.fori_loop(0, tm, start, 0, unroll=8)
    for _ in range(2):
        pltpu.make_async_copy(h_ref, xs_out.at[pl.ds(0, tm)], sem).wait()


def _dispatch(h, pos0, pos1, n_slots, tm):
    n, d = h.shape
    xs0 = jnp.zeros((n_slots, d), h.dtype)
    return pl.pallas_call(
        functools.partial(_dispatch_kernel, tm=tm),
        grid_spec=pltpu.PrefetchScalarGridSpec(
            num_scalar_prefetch=2,
            grid=(n // tm,),
            in_specs=[pl.BlockSpec((tm, d), lambda i, p0, p1: (i, 0)),
                      pl.BlockSpec(memory_space=pl.ANY)],
            out_specs=pl.BlockSpec(memory_space=pl.ANY),
            scratch_shapes=[pltpu.SemaphoreType.DMA(())],
        ),
        out_shape=jax.ShapeDtypeStruct((n_slots, d), h.dtype),
        input_output_aliases={3: 0},
        compiler_params=_params(("arbitrary",)),
        name="dispatch_rows",
    )(pos0, pos1, h, xs0)


def _combine_kernel(pos0_ref, pos1_ref, ys_hbm, x_ref, gate_ref, g_ref, o_ref, buf, sem, *, tm, final):
    base = pl.program_id(0) * tm

    def copies(r):
        t = base + r
        return (pltpu.make_async_copy(ys_hbm.at[pl.ds(pos0_ref[t], 1)], buf.at[0, pl.ds(r, 1)], sem),
                pltpu.make_async_copy(ys_hbm.at[pl.ds(pos1_ref[t], 1)], buf.at[1, pl.ds(r, 1)], sem))

    def start(r, c):
        for cp in copies(r):
            cp.start()
        return c

    lax.fori_loop(0, tm, start, 0, unroll=8)
    for k in range(2):
        pltpu.make_async_copy(ys_hbm.at[pl.ds(0, tm)], buf.at[k], sem).wait()
    gates = gate_ref[...]
    y = x_ref[...] + gates[:, 0:1] * buf[0] + gates[:, 1:2] * buf[1]
    o_ref[...] = _rms(y, g_ref[...]) if final else y


def _combine(ys, x2d, gates, pos0, pos1, g, final, tm):
    n, d = x2d.shape
    return pl.pallas_call(
        functools.partial(_combine_kernel, tm=tm, final=final),
        grid_spec=pltpu.PrefetchScalarGridSpec(
            num_scalar_prefetch=2,
            grid=(n // tm,),
            in_specs=[pl.BlockSpec(memory_space=pl.ANY),
                      pl.BlockSpec((tm, d), lambda i, p0, p1: (i, 0)),
                      pl.BlockSpec((tm, 2), lambda i, p0, p1: (i, 0)),
                      pl.BlockSpec((1, d), lambda i, p0, p1: (0, 0))],
            out_specs=pl.BlockSpec((tm, d), lambda i, p0, p1: (i, 0)),
            scratch_shapes=[pltpu.VMEM((2, tm, d), F32), pltpu.SemaphoreType.DMA(())],
        ),
        out_shape=jax.ShapeDtypeStruct((n, d), F32),
        compiler_params=_params(("arbitrary",)),
        name="combine_rows",
    )(pos0, pos1, ys, x2d, gates, g)


def _final_norm_kernel(x_ref, g_ref, o_ref):
    o_ref[...] = _rms(x_ref[...], g_ref[...])


def _final_norm(x2d, g, tm):
    n, d = x2d.shape
    return pl.pallas_call(
        _final_norm_kernel,
        grid=(n // tm,),
        in_specs=[pl.BlockSpec((tm, d), lambda i: (i, 0)), pl.BlockSpec((1, d), lambda i: (0, 0))],
        out_specs=pl.BlockSpec((tm, d), lambda i: (i, 0)),
        out_shape=jax.ShapeDtypeStruct((n, d), F32),
        compiler_params=_params(("parallel",)),
        name="final_norm",
    )(x2d, g)


def _routing_tables(meta, tm):
    n = meta.shape[1]
    experts = meta[0:2].astype(jnp.int32)
    gates = meta[2:4].T
    flat = experts.reshape(-1)
    onehot = (flat[:, None] == jnp.arange(N_EXPERTS)[None, :]).astype(jnp.int32)
    ranks = jnp.cumsum(onehot, axis=0) - onehot
    rank = jnp.sum(ranks * onehot, axis=1)
    counts = jnp.sum(onehot, axis=0)
    tiles = (counts + tm - 1) // tm
    tile_end = jnp.cumsum(tiles)
    tile_start = tile_end - tiles
    pos = (tile_start * tm)[flat] + rank
    n_tiles = (2 * n) // tm + N_EXPERTS
    tile_ids = jnp.arange(n_tiles)
    tile_valid = (tile_ids < tile_end[-1]).astype(jnp.int32)
    clipped = jnp.minimum(tile_ids, tile_end[-1] - 1)
    tile_expert = jnp.sum((clipped[:, None] >= tile_end[None, :]).astype(jnp.int32), axis=1)
    tile_expert = jnp.minimum(tile_expert, N_EXPERTS - 1).astype(jnp.int32)
    return pos[:n].astype(jnp.int32), pos[n:].astype(jnp.int32), gates, tile_expert, tile_valid


TM_IN, TN_IN = 512, 2816
TQ_SB, SB_HEADS_PER_STEP = 256, 6
TQ_DA, TK_DA = 512, 512
TS_CONV = 512
TM_OUT = 512
TM_FFN, TF_DENSE = 512, 512
TM_EXPERT, TF_EXPERT = 1024, 256
TM_ROWS = 512


def kernel(x, attn_norm, w_in, w_out, lam, diff_norm, conv_w, conv_b, conv_ln_g, conv_ln_b, w_conv_out, ffn_norm, w_gate, w_up, w_down, w_router, e_gate, e_up, e_down, final_norm):
    b, s, d = x.shape
    n = b * s
    depth = w_in.shape[0]
    sb_w = SB_HEADS * HEAD_DIM
    da_w = DA_HEADS * HEAD_DIM
    cv_w = conv_w.shape[2]
    in_w = w_in.shape[2]

    colscale = jnp.ones((in_w,), F32)
    colscale = colscale.at[:sb_w].set(HEAD_DIM ** -0.5)
    da_q0 = in_w - 3 * da_w
    colscale = colscale.at[da_q0:da_q0 + da_w].set(DA_HALF ** -0.5 * LOG2E).reshape(1, in_w)
    slopes = jnp.exp2(-8.0 * (jnp.arange(DA_HEADS, dtype=F32) + 1.0) / DA_HEADS) * LOG2E

    x2d = x.reshape(n, d)
    out = None
    for l in range(depth):
        lam_init = 0.8 - 0.6 * math.exp(-0.3 * l)
        moe = l % 2 == 1
        i = l // 2
        proj = _norm_in_proj(x2d, attn_norm[l].reshape(1, d), w_in[l].astype(BF16), colscale,
                             TM_IN, TN_IN).reshape(b, s, in_w)
        sb_o = _sb_attention(proj, TQ_SB, SB_HEADS_PER_STEP).reshape(n, sb_w)
        cv_o = _conformer_conv(proj, conv_w[l], conv_b[l], conv_ln_g[l], conv_ln_b[l],
                               w_conv_out[l].astype(BF16), TS_CONV).reshape(n, cv_w)
        da_o = _da_attention(proj, lam[l], diff_norm[l], slopes, lam_init, TQ_DA, TK_DA).reshape(n, da_w)
        x2d, h, *meta = _out_proj(sb_o, cv_o, da_o, x2d, w_out[l].astype(BF16), ffn_norm[l].reshape(1, d),
                                  F32 if moe else BF16, TM_OUT, w_router[i] if moe else None)
        if not moe:
            ones = jnp.ones((n // TM_FFN,), jnp.int32)
            x2d = _ffn(h, x2d, w_gate[i].astype(BF16)[None], w_up[i].astype(BF16)[None],
                       w_down[i].astype(BF16)[None], jnp.zeros_like(ones), ones, TM_FFN, TF_DENSE)
            if l == depth - 1:
                out = _final_norm(x2d, final_norm.reshape(1, d), TM_ROWS)
        else:
            pos0, pos1, gates, tile_expert, tile_valid = _routing_tables(meta[0], TM_EXPERT)
            n_slots = 2 * n + N_EXPERTS * TM_EXPERT
            xs = _dispatch(h, pos0, pos1, n_slots, TM_ROWS)
            ys = _ffn(xs, None, e_gate[i].astype(BF16), e_up[i].astype(BF16), e_down[i].astype(BF16),
                      tile_expert, tile_valid, TM_EXPERT, TF_EXPERT)
            last = l == depth - 1
            x2d = _combine(ys, x2d, gates, pos0, pos1, final_norm.reshape(1, d), last, TM_ROWS)
            if last:
                out = x2d
    return out.reshape(b, s, d)
```

```python
import functools
import math

import jax
import jax.numpy as jnp
from jax import lax
from jax.experimental import pallas as pl
from jax.experimental.pallas import tpu as pltpu

F32 = jnp.float32
BF16 = jnp.bfloat16

HEAD_DIM = 128
SB_HEADS = 6
DA_HEADS = 6
DA_HALF = HEAD_DIM // 2
CHUNK = 64
CONV_KERNEL = 31
CONV_HALO = 32
SUBLANES = 8
CONV_ROWS = 64
N_EXPERTS = 8
EPS = 1e-6
NEG_BIG = -1e30
SB_EXIT = 104.0
LOG2E = 1.4426950408889634
DA_SUM_ROWS = 16

VMEM_LIMIT = 56 * 1024 * 1024


def _params(sem, vmem=VMEM_LIMIT):
    return pltpu.CompilerParams(dimension_semantics=sem, vmem_limit_bytes=vmem)


def _rms(x, g):
    ms = jnp.mean(x * x, axis=-1, keepdims=True)
    return x * lax.rsqrt(ms + EPS) * g


def _dot(a, b):
    return jnp.dot(a, b, preferred_element_type=F32)


def _dot_nt(a, b):
    return lax.dot_general(a, b, (((1,), (1,)), ((), ())), preferred_element_type=F32)


def _norm_matmul_kernel(x_ref, g_ref, w_ref, cs_ref, o_ref, h_scr):
    @pl.when(pl.program_id(1) == 0)
    def _():
        h_scr[...] = _rms(x_ref[...], g_ref[...]).astype(BF16)

    o_ref[...] = (_dot(h_scr[...], w_ref[...]) * cs_ref[...]).astype(o_ref.dtype)


def _norm_in_proj(x2d, g, w, colscale, tm, tn):
    n, d = x2d.shape
    nout = w.shape[1]
    return pl.pallas_call(
        _norm_matmul_kernel,
        grid=(n // tm, nout // tn),
        in_specs=[
            pl.BlockSpec((tm, d), lambda i, j: (i, 0)),
            pl.BlockSpec((1, d), lambda i, j: (0, 0)),
            pl.BlockSpec((d, tn), lambda i, j: (0, j)),
            pl.BlockSpec((1, tn), lambda i, j: (0, j)),
        ],
        out_specs=pl.BlockSpec((tm, tn), lambda i, j: (i, j)),
        out_shape=jax.ShapeDtypeStruct((n, nout), BF16),
        scratch_shapes=[pltpu.VMEM((tm, d), BF16)],
        compiler_params=_params(("parallel", "arbitrary")),
        name="norm_in_proj",
    )(x2d, g, w, colscale)


def _sb_kernel(q_ref, k_ref, v_ref, o_ref, *, tq, heads):
    qi = pl.program_id(2)
    row = lax.broadcasted_iota(jnp.int32, (tq, tq), 0)
    col = lax.broadcasted_iota(jnp.int32, (tq, tq), 1)
    tri = jnp.where(row > col, 1.0, 0.0).astype(BF16)
    causal = col < row
    lanes = [slice(h * HEAD_DIM, (h + 1) * HEAD_DIM) for h in range(heads)]
    q = [q_ref[:, ln] for ln in lanes]

    def log1m_sigmoid(z):
        return jnp.minimum(-z, 0.0) - jnp.log(1.0 + jnp.exp(-jnp.abs(z)))

    def later_sum(l1m):
        hi = l1m.astype(BF16)
        lo = (l1m - hi.astype(F32)).astype(BF16)
        return _dot(hi, tri) + _dot(lo, tri)

    def block(h, kb, rem, acc, diag):
        start = pl.multiple_of(kb * tq, tq)
        k = k_ref[pl.ds(start, tq), lanes[h]]
        v = v_ref[pl.ds(start, tq), lanes[h]]
        z = _dot_nt(q[h], k)
        l1m = log1m_sigmoid(z)
        if diag:
            l1m = jnp.where(causal, l1m, 0.0)
        after = rem + later_sum(l1m)
        w = jnp.exp(l1m + z + after)
        if diag:
            w = jnp.where(causal, w, 0.0)
        acc = acc + _dot(w.astype(BF16), v)
        rem = rem + jnp.sum(l1m, axis=1, keepdims=True)
        return rem, acc

    rem0 = jnp.zeros((tq, 1), F32)
    acc0 = jnp.zeros((tq, HEAD_DIM), F32)

    @pl.when(qi == 0)
    def _():
        for h in range(heads):
            _, acc = block(h, 0, rem0, acc0, True)
            o_ref[:, lanes[h]] = acc.astype(o_ref.dtype)

    def own_and_previous_block(h):
        start = pl.multiple_of((qi - 1) * tq, tq)
        z = _dot_nt(q[h], k_ref[pl.ds(start, 2 * tq), lanes[h]])
        l1m = log1m_sigmoid(z)
        l_prev = l1m[:, :tq]
        l_own = jnp.where(causal, l1m[:, tq:], 0.0)
        sum_own = jnp.sum(l_own, axis=1, keepdims=True)
        w_own = jnp.where(causal, jnp.exp(l_own + z[:, tq:] + later_sum(l_own)), 0.0)
        w_prev = jnp.exp(l_prev + z[:, :tq] + (later_sum(l_prev) + sum_own))
        w = jnp.concatenate([w_prev, w_own], axis=1).astype(BF16)
        acc = _dot(w, v_ref[pl.ds(start, 2 * tq), lanes[h]])
        return sum_own + jnp.sum(l_prev, axis=1, keepdims=True), acc

    @pl.when(qi > 0)
    def _():
        newest = [own_and_previous_block(h) for h in range(heads)]
        for h, (rem, acc) in enumerate(newest):
            def cond(c):
                return jnp.logical_and(c[0] >= 0, c[1] > -SB_EXIT)

            def body(c, h=h):
                r, a = block(h, c[0], c[2], c[3], False)
                return c[0] - 1, jnp.max(r), r, a

            _, _, _, acc = lax.while_loop(cond, body, (qi - 2, jnp.max(rem), rem, acc))
            o_ref[:, lanes[h]] = acc.astype(o_ref.dtype)


def _sb_attention(proj, tq, heads):
    b, s, _ = proj.shape
    groups = SB_HEADS // heads
    width = heads * HEAD_DIM
    q_off, k_off, v_off = 0, groups, 2 * groups
    return pl.pallas_call(
        functools.partial(_sb_kernel, tq=tq, heads=heads),
        grid=(b, groups, s // tq),
        in_specs=[
            pl.BlockSpec((None, tq, width), lambda bi, g, i: (bi, i, q_off + g)),
            pl.BlockSpec((None, s, width), lambda bi, g, i: (bi, 0, k_off + g)),
            pl.BlockSpec((None, s, width), lambda bi, g, i: (bi, 0, v_off + g)),
        ],
        out_specs=pl.BlockSpec((None, tq, width), lambda bi, g, i: (bi, i, g)),
        out_shape=jax.ShapeDtypeStruct((b, s, SB_HEADS * HEAD_DIM), BF16),
        compiler_params=_params(("parallel", "parallel", "arbitrary")),
        name="sb_attention",
    )(proj, proj, proj)


def _da_position_terms(pos, ns, key_side):
    axis = 1 if key_side else 0
    idx = lax.broadcasted_iota(jnp.int32, pos.shape, axis)
    nh = ns.astype(BF16).astype(F32)
    nl = (ns - nh).astype(BF16).astype(F32)
    hi = (pos // CHUNK).astype(F32)
    lo = (pos % CHUNK).astype(F32)
    sign = 1.0 if key_side else -1.0
    coef = jnp.where(idx % 2 == 0, nh, nl) * jnp.where(idx % 4 < 2, sign * CHUNK, sign)
    posv = jnp.where(idx % 4 < 2, hi, lo)
    first = (idx < 4) if key_side else (idx >= 4)
    return jnp.where(idx < 8, jnp.where(first, coef, posv), 0.0).astype(BF16)


def _da_kernel(slopes_ref, lam_ref, q_ref, k_ref, v_ref, g_ref, o_ref, kaug, vaug, sa, sb, own_fix, *, tq, tk,
               lam_init):
    h = pl.program_id(1)
    qi = pl.program_id(2)
    nslope = -slopes_ref[h]
    nkb, rows, _ = vaug.shape

    @pl.when(qi == 0)
    def _():
        sub = lax.broadcasted_iota(jnp.int32, (rows - HEAD_DIM, tk), 0)
        tail = jnp.where(sub == 0, 1.0, 0.0).astype(BF16)
        ns = jnp.full((tk, HEAD_DIM), nslope, F32)
        for c in range(nkb):
            vaug[c, 0:HEAD_DIM, :] = v_ref[c * tk:(c + 1) * tk, :].astype(F32).T.astype(BF16)
            vaug[c, HEAD_DIM:rows, :] = tail
            kaug[c * tk:(c + 1) * tk, 0:HEAD_DIM] = k_ref[c * tk:(c + 1) * tk, :]
            pos = c * tk + lax.broadcasted_iota(jnp.int32, (tk, HEAD_DIM), 0)
            kaug[c * tk:(c + 1) * tk, HEAD_DIM:] = _da_position_terms(pos, ns, True)
        jj = lax.broadcasted_iota(jnp.int32, (tk, tq), 0)
        ii = lax.broadcasted_iota(jnp.int32, (tk, tq), 1)
        for v in range(tk // tq):
            allowed = (jj // CHUNK) <= ((ii + v * tq) // CHUNK)
            d = (ii + v * tq - jj).astype(F32)
            own_fix[v] = jnp.where(allowed, jnp.where(d < 0.0, -2.0 * nslope * d, 0.0), NEG_BIG)

    qt = q_ref[...].astype(F32).T
    sub = lax.broadcasted_iota(jnp.int32, (HEAD_DIM, tq), 0)
    tpos = qi * tq + lax.broadcasted_iota(jnp.int32, (HEAD_DIM, tq), 1)
    qpos = _da_position_terms(tpos, jnp.full((HEAD_DIM, tq), nslope, F32), False)
    q1 = jnp.concatenate([jnp.where(sub < DA_HALF, qt, 0.0).astype(BF16), qpos], axis=0)
    q2 = jnp.concatenate([jnp.where(sub >= DA_HALF, qt, 0.0).astype(BF16), qpos], axis=0)
    kd = (qi * tq) // tk
    rel = qi * tq - kd * tk

    def update(st, m, acc, vt):
        m_new = jnp.maximum(m, jnp.max(st, axis=0, keepdims=True))
        alpha = jnp.exp2(m - m_new)
        p = jnp.exp2(st - m_new).astype(BF16)
        acc = alpha * acc + _dot(vt, p)
        return m_new, acc

    def put_scores(s_ref, kb):
        k = kaug[pl.ds(pl.multiple_of(kb * tk, tk), tk), :]
        s_ref[0] = _dot(k, q1)
        s_ref[1] = _dot(k, q2)

    def block(kb, s_ref, carry, diag):
        m1, a1, m2, a2 = carry
        vt = vaug[kb]
        s1 = s_ref[0]
        s2 = s_ref[1]
        if diag:
            fix = own_fix[rel // tq]
            s1 = s1 + fix
            s2 = s2 + fix
        m1, a1 = update(s1, m1, a1, vt)
        m2, a2 = update(s2, m2, a2, vt)
        return m1, a1, m2, a2

    m0 = jnp.full((1, tq), NEG_BIG, F32)
    a0 = jnp.zeros((rows, tq), F32)
    put_scores(sa, kd)
    put_scores(sb, jnp.maximum(kd - 1, 0))
    carry = block(kd, sa, (m0, a0, m0, a0), True)

    def odd_block(c):
        c = block(kd - 1, sb, c, False)
        put_scores(sb, jnp.maximum(kd - 2, 0))
        return c

    carry = lax.cond(kd % 2 == 1, odd_block, lambda c: c, carry)
    nb = kd - 1 - kd % 2

    def pair(j, c):
        kb = nb - 2 * j
        put_scores(sa, kb - 1)
        c = block(kb, sb, c, False)
        put_scores(sb, jnp.maximum(kb - 2, 0))
        return block(kb - 1, sa, c, False)

    _, a1, _, a2 = lax.fori_loop(0, (nb + 1) // 2, pair, carry)

    lv = lam_ref[...]
    lam_full = (jnp.exp(jnp.sum(lv[0:1] * lv[1:2], axis=1, keepdims=True))
                - jnp.exp(jnp.sum(lv[2:3] * lv[3:4], axis=1, keepdims=True)) + lam_init)
    o = (a1[:HEAD_DIM] / a1[HEAD_DIM:HEAD_DIM + 1]
         - lam_full * (a2[:HEAD_DIM] / a2[HEAD_DIM:HEAD_DIM + 1]))
    ms = jnp.mean(o * o, axis=0, keepdims=True)
    on = o * lax.rsqrt(ms + EPS) * g_ref[...] * (1.0 - lam_init)
    o_ref[...] = on.T.astype(o_ref.dtype)


def _da_attention(proj, lam_l, diff_norm_l, slopes, lam_init, tq, tk):
    b, s, width = proj.shape
    nblk = width // HEAD_DIM
    q_off, k_off, v_off = nblk - 3 * DA_HEADS, nblk - 2 * DA_HEADS, nblk - DA_HEADS
    return pl.pallas_call(
        functools.partial(_da_kernel, tq=tq, tk=tk, lam_init=lam_init),
        grid_spec=pltpu.PrefetchScalarGridSpec(
            num_scalar_prefetch=1,
            grid=(b, DA_HEADS, s // tq),
            in_specs=[
                pl.BlockSpec((4, DA_HALF), lambda bi, h, i, sl: (0, 0)),
                pl.BlockSpec((None, tq, HEAD_DIM), lambda bi, h, i, sl: (bi, i, q_off + h)),
                pl.BlockSpec((None, s, HEAD_DIM), lambda bi, h, i, sl: (bi, 0, k_off + h)),
                pl.BlockSpec((None, s, HEAD_DIM), lambda bi, h, i, sl: (bi, 0, v_off + h)),
                pl.BlockSpec((HEAD_DIM, 1), lambda bi, h, i, sl: (h, 0)),
            ],
            out_specs=pl.BlockSpec((None, tq, HEAD_DIM), lambda bi, h, i, sl: (bi, i, h)),
            scratch_shapes=[pltpu.VMEM((s, 2 * HEAD_DIM), BF16),
                            pltpu.VMEM((s // tk, HEAD_DIM + DA_SUM_ROWS, tk), BF16),
                            pltpu.VMEM((2, tk, tq), F32), pltpu.VMEM((2, tk, tq), F32),
                            pltpu.VMEM((tk // tq, tk, tq), F32)],
        ),
        out_shape=jax.ShapeDtypeStruct((b, s, DA_HEADS * HEAD_DIM), BF16),
        compiler_params=_params(("parallel", "parallel", "arbitrary")),
        name="da_attention",
    )(slopes, lam_l, proj, proj, proj, diff_norm_l.reshape(-1, 1))


def _conv_kernel(a0, a1, g0, g1, ha0, ha1, hg0, hg1, cw_ref, cb_ref, lng_ref, lnb_ref, wpw_ref,
                 o_ref, scr, shf, act, *, ts, half):
    first = pl.program_id(1) == 0
    offs = [CONV_HALO - (CONV_KERNEL - 1) + j for j in range(CONV_KERNEL)]
    for c, (a, g, ha, hg) in enumerate(((a0, g0, ha0, hg0), (a1, g1, ha1, hg1))):
        u = a[...].astype(F32) * jax.nn.sigmoid(g[...].astype(F32))
        hu = ha[...].astype(F32) * jax.nn.sigmoid(hg[...].astype(F32))
        scr[c, 0:CONV_HALO, :] = jnp.where(first, 0.0, hu)
        scr[c, CONV_HALO:CONV_HALO + ts, :] = u
        for r in range(1, SUBLANES):
            last = max((o for o in offs if o % SUBLANES == r), default=None)
            if last is not None:
                span = last - r + ts
                shf[c, r - 1, 0:span, :] = scr[c, r:r + span, :]

    width = 2 * half
    for c0 in range(0, ts, CONV_ROWS):
        conv = []
        for c in range(2):
            lo = c * half
            acc = jnp.broadcast_to(cb_ref[:, lo:lo + half], (CONV_ROWS, half))
            for j, off in enumerate(offs):
                r = off % SUBLANES
                m = off - r + c0
                rows = scr[c, m:m + CONV_ROWS, :] if r == 0 else shf[c, r - 1, m:m + CONV_ROWS, :]
                acc = acc + cw_ref[j:j + 1, lo:lo + half] * rows
            conv.append(acc)
        mu = (jnp.sum(conv[0], axis=1, keepdims=True) + jnp.sum(conv[1], axis=1, keepdims=True)) / width
        d0 = conv[0] - mu
        d1 = conv[1] - mu
        var = (jnp.sum(d0 * d0, axis=1, keepdims=True) + jnp.sum(d1 * d1, axis=1, keepdims=True)) / width
        inv = lax.rsqrt(var + EPS)
        for c, d in enumerate((d0, d1)):
            lo = c * half
            un = d * inv * lng_ref[:, lo:lo + half] + lnb_ref[:, lo:lo + half]
            act[c0:c0 + CONV_ROWS, lo:lo + half] = (un * jax.nn.sigmoid(un)).astype(BF16)
    o_ref[...] = _dot(act[...], wpw_ref[...]).astype(o_ref.dtype)


def _conformer_conv(proj, conv_w, conv_b, ln_g, ln_b, w_pw, ts):
    b, s, _ = proj.shape
    width = conv_w.shape[1]
    half = width // 2
    a_blk = 3 * SB_HEADS * HEAD_DIM // half
    g_blk = a_blk + 2
    hpt = ts // CONV_HALO

    def cur(cb):
        return pl.BlockSpec((None, ts, half), lambda bi, i: (bi, i, cb))

    def halo(cb):
        return pl.BlockSpec((None, CONV_HALO, half),
                            lambda bi, i: (bi, jnp.maximum(i * hpt - 1, 0), cb))

    def full(shape):
        return pl.BlockSpec(shape, lambda bi, i: (0, 0))

    return pl.pallas_call(
        functools.partial(_conv_kernel, ts=ts, half=half),
        grid=(b, s // ts),
        in_specs=[cur(a_blk), cur(a_blk + 1), cur(g_blk), cur(g_blk + 1),
                  halo(a_blk), halo(a_blk + 1), halo(g_blk), halo(g_blk + 1),
                  full((CONV_KERNEL, width)), full((1, width)), full((1, width)), full((1, width)),
                  full((width, width))],
        out_specs=pl.BlockSpec((None, ts, width), lambda bi, i: (bi, i, 0)),
        out_shape=jax.ShapeDtypeStruct((b, s, width), BF16),
        scratch_shapes=[pltpu.VMEM((2, CONV_HALO + ts, half), F32),
                        pltpu.VMEM((2, SUBLANES - 1, CONV_HALO + ts, half), F32),
                        pltpu.VMEM((ts, width), BF16)],
        compiler_params=_params(("parallel", "arbitrary")),
        name="conformer_conv",
    )(proj, proj, proj, proj, proj, proj, proj, proj,
      conv_w, conv_b.reshape(1, -1), ln_g.reshape(1, -1), ln_b.reshape(1, -1), w_pw)


def _top2_meta(logits, n_experts, rows):
    e = lax.broadcasted_iota(jnp.int32, logits.shape, 0)
    logits = jnp.where(e < n_experts, logits, -jnp.inf)
    m1 = jnp.max(logits, axis=0, keepdims=True)
    i1 = jnp.min(jnp.where(logits == m1, e, n_experts), axis=0, keepdims=True)
    rest = jnp.where(e == i1, -jnp.inf, logits)
    m2 = jnp.max(rest, axis=0, keepdims=True)
    i2 = jnp.min(jnp.where(rest == m2, e, n_experts), axis=0, keepdims=True)
    t = jnp.exp(m2 - m1)
    g1 = 1.0 / (1.0 + t)
    g2 = t / (1.0 + t)
    r = lax.broadcasted_iota(jnp.int32, (rows, logits.shape[1]), 0)
    meta = jnp.where(r == 0, i1.astype(F32), 0.0)
    meta = jnp.where(r == 1, i2.astype(F32), meta)
    meta = jnp.where(r == 2, g1, meta)
    return jnp.where(r == 3, g2, meta)


def _out_proj_kernel(sb_ref, cv_ref, da_ref, x_ref, w_ref, g_ref, *rest, w_sb, w_cv, n_experts):
    if n_experts:
        wr_ref, xo_ref, ho_ref, meta_ref = rest
    else:
        xo_ref, ho_ref = rest
    acc = _dot(sb_ref[...], w_ref[0:w_sb, :])
    acc = acc + _dot(cv_ref[...], w_ref[w_sb:w_sb + w_cv, :])
    acc = acc + _dot(da_ref[...], w_ref[w_sb + w_cv:, :])
    xn = x_ref[...] + acc
    xo_ref[...] = xn
    h = _rms(xn, g_ref[...])
    ho_ref[...] = h.astype(ho_ref.dtype)
    if n_experts:
        logits = _dot_nt(wr_ref[...], h.astype(BF16))
        meta_ref[...] = _top2_meta(logits, n_experts, meta_ref.shape[0])


def _out_proj(sb_o, cv_o, da_o, x2d, w_out, g, h_dtype, tm, w_router=None):
    n, d = x2d.shape
    w_sb, w_cv, w_da = sb_o.shape[1], cv_o.shape[1], da_o.shape[1]
    row = lambda width: pl.BlockSpec((tm, width), lambda i: (i, 0))
    in_specs = [row(w_sb), row(w_cv), row(w_da), row(d),
                pl.BlockSpec((d, d), lambda i: (0, 0)),
                pl.BlockSpec((1, d), lambda i: (0, 0))]
    out_specs = [row(d), row(d)]
    out_shape = [jax.ShapeDtypeStruct((n, d), F32), jax.ShapeDtypeStruct((n, d), h_dtype)]
    args = [sb_o, cv_o, da_o, x2d, w_out, g]
    n_experts = 0
    if w_router is not None:
        n_experts = w_router.shape[1]
        args.append(jnp.zeros((16, d), BF16).at[:n_experts].set(w_router.T.astype(BF16)))
        in_specs.append(pl.BlockSpec((16, d), lambda i: (0, 0)))
        out_specs.append(pl.BlockSpec((8, tm), lambda i: (0, i)))
        out_shape.append(jax.ShapeDtypeStruct((8, n), F32))
    return pl.pallas_call(
        functools.partial(_out_proj_kernel, w_sb=w_sb, w_cv=w_cv, n_experts=n_experts),
        grid=(n // tm,),
        in_specs=in_specs,
        out_specs=out_specs,
        out_shape=out_shape,
        compiler_params=_params(("parallel",)),
        name="out_proj",
    )(*args)


def _ffn_kernel(te_ref, tv_ref, x_ref, *rest, residual):
    if residual:
        res_ref, wg_ref, wu_ref, wd_ref, o_ref, xb_scr = rest
    else:
        wg_ref, wu_ref, wd_ref, o_ref, xb_scr = rest
    i = pl.program_id(0)
    f = pl.program_id(1)

    @pl.when(f == 0)
    def _():
        xb_scr[...] = x_ref[...].astype(BF16)
        o_ref[...] = res_ref[...] if residual else jnp.zeros_like(o_ref)

    def swiglu(rows):
        x = xb_scr[0:rows, :]
        g = _dot(x, wg_ref[...])
        u = _dot(x, wu_ref[...])
        a = (g * jax.nn.sigmoid(g) * u).astype(BF16)
        o_ref[0:rows, :] += _dot(a, wd_ref[...])

    half = x_ref.shape[0] // 2

    @pl.when(tv_ref[i] == 2)
    def _():
        swiglu(2 * half)

    @pl.when(tv_ref[i] == 1)
    def _():
        swiglu(half)


def _ffn(x_rows, res, wg, wu, wd, tile_expert, tile_halves, tm, tf):
    rows, d = x_rows.shape
    nf = wg.shape[2] // tf

    def fblk(f, i, tv):
        return jnp.where(tv[i] > 0, f, nf - 1)

    in_specs = [pl.BlockSpec((tm, d), lambda i, f, te, tv: (i, 0))]
    args = [x_rows]
    if res is not None:
        in_specs.append(pl.BlockSpec((tm, d), lambda i, f, te, tv: (i, 0)))
        args.append(res)
    in_specs += [
        pl.BlockSpec((None, d, tf), lambda i, f, te, tv: (te[i], 0, fblk(f, i, tv))),
        pl.BlockSpec((None, d, tf), lambda i, f, te, tv: (te[i], 0, fblk(f, i, tv))),
        pl.BlockSpec((None, tf, d), lambda i, f, te, tv: (te[i], fblk(f, i, tv), 0)),
    ]
    return pl.pallas_call(
        functools.partial(_ffn_kernel, residual=res is not None),
        grid_spec=pltpu.PrefetchScalarGridSpec(
            num_scalar_prefetch=2,
            grid=(rows // tm, nf),
            in_specs=in_specs,
            out_specs=pl.BlockSpec((tm, d), lambda i, f, te, tv: (i, 0)),
            scratch_shapes=[pltpu.VMEM((tm, d), BF16)],
        ),
        out_shape=jax.ShapeDtypeStruct((rows, d), F32),
        compiler_params=_params(("parallel", "arbitrary")),
        name="swiglu_ffn",
    )(tile_expert, tile_halves, *args, wg, wu, wd)


def _dispatch_kernel(pos0_ref, pos1_ref, h_ref, xs_in, xs_out, sem, *, tm):
    del xs_in
    base = pl.program_id(0) * tm

    def copies(r):
        t = base + r
        src = h_ref.at[pl.ds(r, 1)]
        return (pltpu.make_async_copy(src, xs_out.at[pl.ds(pos0_ref[t], 1)], sem),
                pltpu.make_async_copy(src, xs_out.at[pl.ds(pos1_ref[t], 1)], sem))

    def start(r, c):
        for cp in copies(r):
            cp.start()
        return c

    lax.fori_loop(0, tm, start, 0, unroll=8)
    for _ in range(2):
        pltpu.make_async_copy(h_ref, xs_out.at[pl.ds(0, tm)], sem).wait()


def _dispatch(h, pos0, pos1, n_slots, tm):
    n, d = h.shape
    xs0 = jnp.zeros((n_slots, d), h.dtype)
    return pl.pallas_call(
        functools.partial(_dispatch_kernel, tm=tm),
        grid_spec=pltpu.PrefetchScalarGridSpec(
            num_scalar_prefetch=2,
            grid=(n // tm,),
            in_specs=[pl.BlockSpec((tm, d), lambda i, p0, p1: (i, 0)),
                      pl.BlockSpec(memory_space=pl.ANY)],
            out_specs=pl.BlockSpec(memory_space=pl.ANY),
            scratch_shapes=[pltpu.SemaphoreType.DMA(())],
        ),
        out_shape=jax.ShapeDtypeStruct((n_slots, d), h.dtype),
        input_output_aliases={3: 0},
        compiler_params=_params(("arbitrary",)),
        name="dispatch_rows",
    )(pos0, pos1, h, xs0)


def _combine_kernel(pos0_ref, pos1_ref, ys_hbm, x_ref, gate_ref, g_ref, o_ref, buf, sem, *, tm, final):
    base = pl.program_id(0) * tm

    def copies(r):
        t = base + r
        return (pltpu.make_async_copy(ys_hbm.at[pl.ds(pos0_ref[t], 1)], buf.at[0, pl.ds(r, 1)], sem),
                pltpu.make_async_copy(ys_hbm.at[pl.ds(pos1_ref[t], 1)], buf.at[1, pl.ds(r, 1)], sem))

    def start(r, c):
        for cp in copies(r):
            cp.start()
        return c

    lax.fori_loop(0, tm, start, 0, unroll=8)
    for k in range(2):
        pltpu.make_async_copy(ys_hbm.at[pl.ds(0, tm)], buf.at[k], sem).wait()
    gates = gate_ref[...]
    y = x_ref[...] + gates[:, 0:1] * buf[0] + gates[:, 1:2] * buf[1]
    o_ref[...] = _rms(y, g_ref[...]) if final else y


def _combine(ys, x2d, gates, pos0, pos1, g, final, tm):
    n, d = x2d.shape
    return pl.pallas_call(
        functools.partial(_combine_kernel, tm=tm, final=final),
        grid_spec=pltpu.PrefetchScalarGridSpec(
            num_scalar_prefetch=2,
            grid=(n // tm,),
            in_specs=[pl.BlockSpec(memory_space=pl.ANY),
                      pl.BlockSpec((tm, d), lambda i, p0, p1: (i, 0)),
                      pl.BlockSpec((tm, 2), lambda i, p0, p1: (i, 0)),
                      pl.BlockSpec((1, d), lambda i, p0, p1: (0, 0))],
            out_specs=pl.BlockSpec((tm, d), lambda i, p0, p1: (i, 0)),
            scratch_shapes=[pltpu.VMEM((2, tm, d), F32), pltpu.SemaphoreType.DMA(())],
        ),
        out_shape=jax.ShapeDtypeStruct((n, d), F32),
        compiler_params=_params(("arbitrary",)),
        name="combine_rows",
    )(pos0, pos1, ys, x2d, gates, g)


def _final_norm_kernel(x_ref, g_ref, o_ref):
    o_ref[...] = _rms(x_ref[...], g_ref[...])


def _final_norm(x2d, g, tm):
    n, d = x2d.shape
    return pl.pallas_call(
        _final_norm_kernel,
        grid=(n // tm,),
        in_specs=[pl.BlockSpec((tm, d), lambda i: (i, 0)), pl.BlockSpec((1, d), lambda i: (0, 0))],
        out_specs=pl.BlockSpec((tm, d), lambda i: (i, 0)),
        out_shape=jax.ShapeDtypeStruct((n, d), F32),
        compiler_params=_params(("parallel",)),
        name="final_norm",
    )(x2d, g)


def _routing_tables(meta, tm):
    n = meta.shape[1]
    experts = meta[0:2].astype(jnp.int32)
    gates = meta[2:4].T
    flat = experts.reshape(-1)
    onehot = (flat[:, None] == jnp.arange(N_EXPERTS)[None, :]).astype(jnp.int32)
    ranks = jnp.cumsum(onehot, axis=0) - onehot
    rank = jnp.sum(ranks * onehot, axis=1)
    counts = jnp.sum(onehot, axis=0)
    tiles = (counts + tm - 1) // tm
    tile_end = jnp.cumsum(tiles)
    tile_start = tile_end - tiles
    pos = (tile_start * tm)[flat] + rank
    n_tiles = (2 * n) // tm + N_EXPERTS
    tile_ids = jnp.arange(n_tiles)
    in_use = (tile_ids < tile_end[-1]).astype(jnp.int32)
    clipped = jnp.minimum(tile_ids, tile_end[-1] - 1)
    tile_expert = jnp.sum((clipped[:, None] >= tile_end[None, :]).astype(jnp.int32), axis=1)
    tile_expert = jnp.minimum(tile_expert, N_EXPERTS - 1).astype(jnp.int32)
    tile_rows = counts[tile_expert] - (tile_ids - tile_start[tile_expert]) * tm
    tile_rows = jnp.clip(tile_rows, 0, tm) * in_use
    tile_halves = ((tile_rows + tm // 2 - 1) // (tm // 2)).astype(jnp.int32)
    return pos[:n].astype(jnp.int32), pos[n:].astype(jnp.int32), gates, tile_expert, tile_halves


TM_IN, TN_IN = 512, 2816
TQ_SB, SB_HEADS_PER_STEP = 256, 6
TQ_DA, TK_DA = 512, 512
TS_CONV = 512
TM_OUT = 512
TM_FFN, TF_DENSE = 512, 512
TM_EXPERT, TF_EXPERT = 1024, 256
TM_ROWS = 512


def kernel(x, attn_norm, w_in, w_out, lam, diff_norm, conv_w, conv_b, conv_ln_g, conv_ln_b, w_conv_out, ffn_norm, w_gate, w_up, w_down, w_router, e_gate, e_up, e_down, final_norm):
    b, s, d = x.shape
    n = b * s
    depth = w_in.shape[0]
    sb_w = SB_HEADS * HEAD_DIM
    da_w = DA_HEADS * HEAD_DIM
    cv_w = conv_w.shape[2]
    in_w = w_in.shape[2]

    colscale = jnp.ones((in_w,), F32)
    colscale = colscale.at[:sb_w].set(HEAD_DIM ** -0.5)
    da_q0 = in_w - 3 * da_w
    colscale = colscale.at[da_q0:da_q0 + da_w].set(DA_HALF ** -0.5 * LOG2E).reshape(1, in_w)
    slopes = jnp.exp2(-8.0 * (jnp.arange(DA_HEADS, dtype=F32) + 1.0) / DA_HEADS) * LOG2E

    x2d = x.reshape(n, d)
    out = None
    for l in range(depth):
        lam_init = 0.8 - 0.6 * math.exp(-0.3 * l)
        moe = l % 2 == 1
        i = l // 2
        proj = _norm_in_proj(x2d, attn_norm[l].reshape(1, d), w_in[l].astype(BF16), colscale,
                             TM_IN, TN_IN).reshape(b, s, in_w)
        sb_o = _sb_attention(proj, TQ_SB, SB_HEADS_PER_STEP).reshape(n, sb_w)
        cv_o = _conformer_conv(proj, conv_w[l], conv_b[l], conv_ln_g[l], conv_ln_b[l],
                               w_conv_out[l].astype(BF16), TS_CONV).reshape(n, cv_w)
        da_o = _da_attention(proj, lam[l], diff_norm[l], slopes, lam_init, TQ_DA, TK_DA).reshape(n, da_w)
        x2d, h, *meta = _out_proj(sb_o, cv_o, da_o, x2d, w_out[l].astype(BF16), ffn_norm[l].reshape(1, d),
                                  F32 if moe else BF16, TM_OUT, w_router[i] if moe else None)
        if not moe:
            full = jnp.full((n // TM_FFN,), 2, jnp.int32)
            x2d = _ffn(h, x2d, w_gate[i].astype(BF16)[None], w_up[i].astype(BF16)[None],
                       w_down[i].astype(BF16)[None], jnp.zeros_like(full), full, TM_FFN, TF_DENSE)
            if l == depth - 1:
                out = _final_norm(x2d, final_norm.reshape(1, d), TM_ROWS)
        else:
            pos0, pos1, gates, tile_expert, tile_halves = _routing_tables(meta[0], TM_EXPERT)
            n_slots = 2 * n + N_EXPERTS * TM_EXPERT
            xs = _dispatch(h, pos0, pos1, n_slots, TM_ROWS)
            ys = _ffn(xs, None, e_gate[i].astype(BF16), e_up[i].astype(BF16), e_down[i].astype(BF16),
                      tile_expert, tile_halves, TM_EXPERT, TF_EXPERT)
            last = l == depth - 1
            x2d = _combine(ys, x2d, gates, pos0, pos1, final_norm.reshape(1, d), last, TM_ROWS)
            if last:
                out = x2d
    return out.reshape(b, s, d)
```
